```python
import functools
import jax, jax.numpy as jnp
from jax import lax
import numpy as np

D_MODEL = 1024
BATCH = 1
SEQ = 16384
DEPTH = 1
DEC_BATCH = 128
DEC_SEQ = 4
PAST_LEN = 8192
PAGE_SIZE = 128

N_META = 16
N_HEADS = 8
HEAD_DIM = 64
KV_HEADS = 4
ATTN_DIM = N_HEADS * HEAD_DIM
IDX_HEADS = 8
IDX_DIM = 64
TOPK_MAX = 256
Q_BLOCK = 128
CONV_DIM = D_MODEL // 2
CONV_WIDTH = 3
N_GROUPS = 4
EXPERTS_PER_GROUP = 4
N_EXPERTS = N_GROUPS * EXPERTS_PER_GROUP
TOP_K_IN_GROUP = 2
D_EXPERT = D_MODEL // 4
ROPE_THETA = 10000.0
LN_EPS = 1e-5
DEEPNORM_ALPHA = (2 * DEPTH) ** 0.25
DEEPNORM_BETA = (8 * DEPTH) ** -0.25
IDX_W_SCALE = (IDX_HEADS ** -0.5) * (IDX_DIM ** -0.5)
SPLIT_SIZES = (ATTN_DIM, KV_HEADS * HEAD_DIM, KV_HEADS * HEAD_DIM, IDX_HEADS * IDX_DIM, IDX_HEADS, IDX_DIM,
               CONV_DIM, CONV_DIM, CONV_DIM, D_MODEL, D_MODEL)
PROJ_DIM = sum(SPLIT_SIZES)

kernel_name = 'hybrid_dsa_shortconv_hiermoe_step'


def layer_norm(x, g, b, out_dtype):
    xf = x.astype(jnp.float32)
    mu = jnp.mean(xf, axis=-1, keepdims=True)
    var = jnp.mean(jnp.square(xf - mu), axis=-1, keepdims=True)
    y = (xf - mu) * lax.rsqrt(var + LN_EPS) * g.astype(jnp.float32) + b.astype(jnp.float32)
    return y.astype(out_dtype)


def rope(x, pos):
    d = x.shape[-1]
    half = d // 2
    inv = jnp.power(jnp.float32(ROPE_THETA), -jnp.arange(half, dtype=jnp.float32) * 2.0 / d)
    ang = pos.astype(jnp.float32)[:, None] * inv[None, :]
    cos = jnp.cos(ang)[:, None, :]
    sin = jnp.sin(ang)[:, None, :]
    xf = x.astype(jnp.float32)
    x1, x2 = xf[..., :half], xf[..., half:]
    return jnp.concatenate([x1 * cos - x2 * sin, x1 * sin + x2 * cos], axis=-1).astype(x.dtype)


def take_rows(a, idx):
    return jax.vmap(lambda a_n, i_n: a_n[i_n])(a, idx)


def indexer_scores(iq, iw, ik):
    dots = jnp.einsum('nqhd,nld->nqhl', iq, ik).astype(jnp.float32)
    return jnp.einsum('nqh,nqhl->nql', iw.astype(jnp.float32), jax.nn.relu(dots))


def sparse_attend(q, ksel, vsel, valid):
    n, nq = q.shape[:2]
    qg = q.reshape(n, nq, KV_HEADS, N_HEADS // KV_HEADS, HEAD_DIM)
    s = jnp.einsum('nqhgd,nqkhd->nqhgk', qg, ksel).astype(jnp.float32) * (HEAD_DIM ** -0.5)
    s = jnp.where(valid[:, :, None, None, :], s, -jnp.inf)
    p = jax.nn.softmax(s, axis=-1).astype(vsel.dtype)
    o = jnp.einsum('nqhgk,nqkhd->nqhgd', p, vsel)
    return o.reshape(n, nq, ATTN_DIM)


def prompt_sparse_attention(q, k, v, iq, iw, ik, *, top_k):
    n, t = q.shape[:2]
    nb = -(-t // Q_BLOCK)
    pad = nb * Q_BLOCK - t

    def to_blocks(a):
        a = jnp.pad(a, [(0, 0), (0, pad)] + [(0, 0)] * (a.ndim - 2))
        return jnp.swapaxes(a.reshape((n, nb, Q_BLOCK) + a.shape[2:]), 0, 1)

    kpos = jnp.arange(t, dtype=jnp.int32)
    qpos_b = jnp.arange(nb * Q_BLOCK, dtype=jnp.int32).reshape(nb, Q_BLOCK)

    def block(args):
        qb, iqb, iwb, qpos = args
        sc = indexer_scores(iqb, iwb, ik)
        sc = jnp.where((kpos[None, :] <= qpos[:, None])[None], sc, -jnp.inf)
        _, sel = lax.top_k(sc, top_k)
        valid = sel <= qpos[None, :, None]
        return sparse_attend(qb, take_rows(k, sel), take_rows(v, sel), valid)

    out = lax.map(block, (to_blocks(q), to_blocks(iq), to_blocks(iw), qpos_b))
    return jnp.swapaxes(out, 0, 1).reshape(n, nb * Q_BLOCK, ATTN_DIM)[:, :t]


def sample_sparse_attention(q, k, v, iq, iw, ik, *, cache_k, cache_v, cache_ik, layer, page_table, top_k):
    db, s_new = q.shape[:2]
    page = cache_k.shape[2]
    past_len = page_table.shape[1] * page
    past_ik = cache_ik[layer, page_table].reshape(db, past_len, IDX_DIM)
    ik_all = jnp.concatenate([past_ik, ik.astype(past_ik.dtype)], axis=1)
    qpos = past_len + jnp.arange(s_new, dtype=jnp.int32)
    kpos = jnp.arange(past_len + s_new, dtype=jnp.int32)
    sc = indexer_scores(iq, iw, ik_all)
    sc = jnp.where((kpos[None, :] <= qpos[:, None])[None], sc, -jnp.inf)
    _, sel = lax.top_k(sc, top_k)
    past_idx = jnp.minimum(sel, past_len - 1)
    phys = jnp.take_along_axis(page_table, (past_idx // page).reshape(db, -1), axis=1).reshape(sel.shape)
    off = past_idx % page
    is_new = (sel >= past_len)[..., None, None]
    new_idx = jnp.clip(sel - past_len, 0, s_new - 1)

    def pick(cache, cur):
        old = cache[layer, phys, off]
        return jnp.where(is_new, take_rows(cur, new_idx), old)

    valid = sel <= qpos[None, :, None]
    return sparse_attend(q, pick(cache_k, k), pick(cache_v, v), valid)


def short_conv(cu, cb, cc, w_conv, prev):
    z = cc * cu
    zp = jnp.concatenate([prev.astype(z.dtype), z], axis=1)
    t = z.shape[1]
    y = w_conv[0] * zp[:, 0:t]
    for j in range(1, CONV_WIDTH):
        y = y + w_conv[j] * zp[:, j:j + t]
    return cb * y, zp[:, -(CONV_WIDTH - 1):]


def hier_moe(h, w_group, b_group, w_er, b_er, w_gate, w_up, w_down):
    n, t, d = h.shape
    xf = h.reshape(n * t, d)
    g_logits = (xf @ w_group + b_group).astype(jnp.float32)
    g_sel = jnp.argmax(g_logits, axis=-1)
    g_p = jnp.take_along_axis(jax.nn.softmax(g_logits, axis=-1), g_sel[:, None], axis=1)
    e_logits = (xf @ w_er + b_er).astype(jnp.float32).reshape(-1, N_GROUPS, EXPERTS_PER_GROUP)
    e_logits = jnp.take_along_axis(e_logits, g_sel[:, None, None], axis=1)[:, 0]
    top_p, top_i = lax.top_k(jax.nn.softmax(e_logits, axis=-1), TOP_K_IN_GROUP)
    top_p = top_p / jnp.sum(top_p, axis=-1, keepdims=True) * g_p
    ids = g_sel[:, None] * EXPERTS_PER_GROUP + top_i
    comb = jnp.sum(jax.nn.one_hot(ids, N_EXPERTS, dtype=jnp.float32) * top_p[..., None], axis=1)
    hg = jnp.einsum('md,edf->mef', xf, w_gate)
    hu = jnp.einsum('md,edf->mef', xf, w_up)
    act = jax.nn.silu(hg) * hu * comb[:, :, None].astype(hg.dtype)
    out = jnp.einsum('mef,efd->md', act, w_down)
    return out.reshape(n, t, d).astype(h.dtype)


def trunk_layer(h, pos, conv_prev, attend, lw):
    (w_in, b_in, w_conv, w_attn_up, w_conv_out, w_o, ln1_g, ln1_b,
     w_group, b_group, w_er, b_er, w_gate, w_up, w_down, ln2_g, ln2_b) = lw
    n, t, _ = h.shape
    p = jnp.einsum('ntd,de->nte', h, w_in) + b_in
    offs = np.cumsum(SPLIT_SIZES)[:-1].tolist()
    q, k, v, iq, iw, ik, cu, cb, cc, ga, gb = jnp.split(p, offs, axis=-1)
    q = rope(q.reshape(n, t, N_HEADS, HEAD_DIM), pos)
    k = rope(k.reshape(n, t, KV_HEADS, HEAD_DIM), pos)
    v = v.reshape(n, t, KV_HEADS, HEAD_DIM)
    iq = rope(iq.reshape(n, t, IDX_HEADS, IDX_DIM), pos)
    ik = rope(ik[:, :, None, :], pos)[:, :, 0, :]
    iw = iw * IDX_W_SCALE
    attn_o = attend(q, k, v, iq, iw, ik)
    conv_o, conv_state = short_conv(cu, cb, cc, w_conv, conv_prev)
    a = attn_o @ w_attn_up
    b = conv_o @ w_conv_out
    mix = (jax.nn.sigmoid(ga) * a + jax.nn.sigmoid(gb) * b) @ w_o
    h = layer_norm(DEEPNORM_ALPHA * h + mix, ln1_g, ln1_b, h.dtype)
    h = layer_norm(DEEPNORM_ALPHA * h + hier_moe(h, w_group, b_group, w_er, b_er, w_gate, w_up, w_down),
                   ln2_g, ln2_b, h.dtype)
    return h, k, v, ik, conv_state


def setup_inputs(seed: int = 0) -> dict:
    key = jax.random.key(seed)
    ks = iter(jax.random.split(key, 32))

    def nrm(shape, scale):
        return jax.random.normal(next(ks), shape, jnp.float32) * scale

    n_pages = PAST_LEN // PAGE_SIZE
    n_phys = (DEC_BATCH * n_pages * 5) // 4
    page_table = jax.random.permutation(next(ks), n_phys)[:DEC_BATCH * n_pages]
    page_table = page_table.reshape(DEC_BATCH, n_pages).astype(jnp.int32)
    v_lo = ATTN_DIM + KV_HEADS * HEAD_DIM
    v_hi = v_lo + KV_HEADS * HEAD_DIM
    w_in = nrm((DEPTH, D_MODEL, PROJ_DIM), D_MODEL ** -0.5)
    w_in = w_in.at[:, :, v_lo:v_hi].multiply(DEEPNORM_BETA)
    return {
        'x_prompt': nrm((BATCH, SEQ, D_MODEL), 1.0),
        'x_sample': nrm((DEC_BATCH, DEC_SEQ, D_MODEL), 1.0),
        'cache_k': nrm((DEPTH, n_phys, PAGE_SIZE, KV_HEADS, HEAD_DIM), 1.0),
        'cache_v': nrm((DEPTH, n_phys, PAGE_SIZE, KV_HEADS, HEAD_DIM), 1.0),
        'cache_idx_k': nrm((DEPTH, n_phys, PAGE_SIZE, IDX_DIM), 1.0),
        'state_conv': nrm((DEPTH, DEC_BATCH, CONV_WIDTH - 1, CONV_DIM), 1.0),
        'page_table': page_table,
        'meta_tokens': nrm((N_META, D_MODEL), 1.0),
        'w_in': w_in,
        'b_in': nrm((DEPTH, PROJ_DIM), 0.02),
        'w_conv': nrm((DEPTH, CONV_WIDTH, CONV_DIM), CONV_WIDTH ** -0.5),
        'w_attn_up': nrm((DEPTH, ATTN_DIM, D_MODEL), DEEPNORM_BETA * ATTN_DIM ** -0.5),
        'w_conv_out': nrm((DEPTH, CONV_DIM, D_MODEL), DEEPNORM_BETA * CONV_DIM ** -0.5),
        'w_o': nrm((DEPTH, D_MODEL, D_MODEL), DEEPNORM_BETA * D_MODEL ** -0.5),
        'ln1_g': 1.0 + nrm((DEPTH, D_MODEL), 0.02),
        'ln1_b': nrm((DEPTH, D_MODEL), 0.02),
        'w_group': nrm((DEPTH, D_MODEL, N_GROUPS), D_MODEL ** -0.5),
        'b_group': nrm((DEPTH, N_GROUPS), 0.01),
        'w_expert_router': nrm((DEPTH, D_MODEL, N_EXPERTS), D_MODEL ** -0.5),
        'b_expert_router': nrm((DEPTH, N_EXPERTS), 0.01),
        'w_gate': nrm((DEPTH, N_EXPERTS, D_MODEL, D_EXPERT), D_MODEL ** -0.5),
        'w_up': nrm((DEPTH, N_EXPERTS, D_MODEL, D_EXPERT), DEEPNORM_BETA * D_MODEL ** -0.5),
        'w_down': nrm((DEPTH, N_EXPERTS, D_EXPERT, D_MODEL), DEEPNORM_BETA * D_EXPERT ** -0.5),
        'ln2_g': 1.0 + nrm((DEPTH, D_MODEL), 0.02),
        'ln2_b': nrm((DEPTH, D_MODEL), 0.02),
    }


def reference(x_prompt, x_sample, cache_k, cache_v, cache_idx_k, state_conv, page_table, meta_tokens,
              w_in, b_in, w_conv, w_attn_up, w_conv_out, w_o, ln1_g, ln1_b,
              w_group, b_group, w_expert_router, b_expert_router, w_gate, w_up, w_down, ln2_g, ln2_b):
    bsz, s_p, d = x_prompt.shape
    s_s = x_sample.shape[1]
    past_len = page_table.shape[1] * cache_k.shape[2]
    t_p = s_p + N_META
    h_p = jnp.concatenate([jnp.broadcast_to(meta_tokens.astype(x_prompt.dtype)[None], (bsz, N_META, d)), x_prompt], axis=1)
    h_s = x_sample
    pos_p = jnp.arange(t_p, dtype=jnp.int32)
    pos_s = past_len + jnp.arange(s_s, dtype=jnp.int32)
    topk_p = min(TOPK_MAX, t_p // 4)
    topk_s = min(TOPK_MAX, (past_len + s_s) // 4)
    conv_zero = jnp.zeros((bsz, CONV_WIDTH - 1, CONV_DIM), x_prompt.dtype)
    kp, vp, ikp, cp, ks, vs, iks, cs = [], [], [], [], [], [], [], []
    for l in range(DEPTH):
        lw = (w_in[l], b_in[l], w_conv[l], w_attn_up[l], w_conv_out[l], w_o[l], ln1_g[l], ln1_b[l],
              w_group[l], b_group[l], w_expert_router[l], b_expert_router[l], w_gate[l], w_up[l], w_down[l],
              ln2_g[l], ln2_b[l])
        attend_p = functools.partial(prompt_sparse_attention, top_k=topk_p)
        h_p, k_l, v_l, ik_l, c_l = trunk_layer(h_p, pos_p, conv_zero, attend_p, lw)
        kp.append(k_l); vp.append(v_l); ikp.append(ik_l); cp.append(c_l)
        attend_s = functools.partial(sample_sparse_attention, cache_k=cache_k, cache_v=cache_v,
                                     cache_ik=cache_idx_k, layer=l, page_table=page_table, top_k=topk_s)
        h_s, k_l, v_l, ik_l, c_l = trunk_layer(h_s, pos_s, state_conv[l], attend_s, lw)
        ks.append(k_l); vs.append(v_l); iks.append(ik_l); cs.append(c_l)
    y_prompt = h_p[:, N_META:]
    return (y_prompt, h_s, jnp.stack(kp), jnp.stack(vp), jnp.stack(ikp), jnp.stack(cp),
            jnp.stack(ks), jnp.stack(vs), jnp.stack(iks), jnp.stack(cs))
```

```python
import functools

import numpy as np
import jax
import jax.numpy as jnp
from jax import lax
from jax.experimental import pallas as pl
from jax.experimental.pallas import tpu as pltpu

F32 = jnp.float32
BF16 = jnp.bfloat16
I32 = jnp.int32

N_META = 16
N_HEADS = 8
HEAD_DIM = 64
KV_HEADS = 4
Q_PER_KV = N_HEADS // KV_HEADS
IDX_HEADS = 8
IDX_DIM = 64
TOPK_MAX = 256
CONV_WIDTH = 3
N_GROUPS = 4
EXPERTS_PER_GROUP = 4
N_EXPERTS = N_GROUPS * EXPERTS_PER_GROUP
ROPE_THETA = 10000.0
LN_EPS = 1e-5
IDX_W_SCALE = (IDX_HEADS ** -0.5) * (IDX_DIM ** -0.5)

LANES = 128
INT_MIN = -(2 ** 31)
KEY_NEG_FLT_MAX = INT_MIN + (1 << 23)
FLT_MAX = float(np.finfo(np.float32).max)
NEG = -1e30
VMEM_LIMIT = 56 * 1024 * 1024

TM = 256
TM_MOE = 512
BQ = 256
PAGES_PER_STEP = 8


def _tile_lanes(x, n):
    reps = n // x.shape[1]
    return x if reps == 1 else jnp.concatenate([x] * reps, axis=1)


def _layer_norm(x, g, b):
    mu = jnp.mean(x, axis=-1, keepdims=True)
    xc = x - mu
    var = jnp.mean(xc * xc, axis=-1, keepdims=True)
    return xc * lax.rsqrt(var + LN_EPS) * g + b


def _proj_layout(d_attn, d_kv, d_idx, d_conv, d_model):
    segs = [("q", d_attn, d_attn), ("k", d_kv, d_kv), ("v", d_kv, d_kv), ("iq", d_idx, d_idx),
            ("iw", IDX_HEADS, LANES), ("ik", IDX_DIM, LANES),
            ("cu", d_conv, d_conv), ("cb", d_conv, d_conv), ("cc", d_conv, d_conv),
            ("ga", d_model, d_model), ("gb", d_model, d_model)]
    offs, o = {}, 0
    for name, _, padded in segs:
        offs[name] = (o, padded)
        o += padded
    return segs, offs, o


def _proj_kernel(x_ref, w_ref, b_ref, cos_ref, sin_ref,
                 q_ref, k_ref, v_ref, iq_ref, ik_ref, iw_ref, z_ref, cb_ref, ga_ref, gb_ref, *, offs):
    xb = x_ref[...].astype(BF16)

    def seg(name):
        a, n = offs[name]
        return jnp.dot(xb, w_ref[:, a:a + n], preferred_element_type=F32) + b_ref[:, a:a + n]

    cos = cos_ref[...]
    sin = sin_ref[...]

    def rope(y):
        n = y.shape[1]
        lane = lax.broadcasted_iota(I32, y.shape, 1)
        first_half = (lane & (HEAD_DIM - 1)) < (HEAD_DIM // 2)
        swapped = jnp.where(first_half, pltpu.roll(y, n - HEAD_DIM // 2, 1), pltpu.roll(y, HEAD_DIM // 2, 1))
        return y * _tile_lanes(cos, n) + swapped * _tile_lanes(sin, n)

    q_ref[...] = rope(seg("q"))
    k_ref[...] = rope(seg("k"))
    v_ref[...] = seg("v")
    iq_ref[...] = rope(seg("iq"))
    ik_ref[...] = rope(seg("ik"))[:, :IDX_DIM]
    iw_ref[...] = seg("iw")[:, :IDX_HEADS] * IDX_W_SCALE
    z_ref[...] = seg("cc") * seg("cu")
    cb_ref[...] = seg("cb")
    ga_ref[...] = seg("ga")
    gb_ref[...] = seg("gb")


def _proj(x, w, b, cos, sin, offs, widths):
    tp, d = x.shape
    n = w.shape[1]
    row = lambda width: pl.BlockSpec((TM, width), lambda i: (i, 0))
    const = lambda shape: pl.BlockSpec(shape, lambda i: (0, 0), pipeline_mode=pl.Buffered(1))
    out_w = [widths["q"], widths["k"], widths["v"], widths["iq"], IDX_DIM, IDX_HEADS,
             widths["cu"], widths["cb"], widths["ga"], widths["gb"]]
    return pl.pallas_call(
        functools.partial(_proj_kernel, offs=offs),
        grid=(tp // TM,),
        in_specs=[row(d), const((d, n)), const((1, n)), row(LANES), row(LANES)],
        out_specs=[row(wd) for wd in out_w],
        out_shape=[jax.ShapeDtypeStruct((tp, wd), F32) for wd in out_w],
        compiler_params=pltpu.CompilerParams(dimension_semantics=("arbitrary",), vmem_limit_bytes=VMEM_LIMIT),
        name="proj",
    )(x, w, b, cos, sin)


def _key_to_float(c):
    bits = jnp.where(c >= 0, c, c ^ jnp.int32(0x7FFFFFFF))
    return lax.bitcast_convert_type(bits, F32)


def _select_topk(s_ref, ntiles, topk, idx_bits):
    _, rows, width = s_ref.shape
    nslab = width // LANES
    lane = lax.broadcasted_iota(I32, (rows, LANES), 1)

    def count(pred):
        def body(t, cnt):
            s = s_ref[t]
            for j in range(nslab):
                cnt = cnt + jnp.where(pred(s[:, j * LANES:(j + 1) * LANES], t * width + j * LANES + lane), 1.0, 0.0)
            return cnt
        cnt = lax.fori_loop(0, ntiles, body, jnp.zeros((rows, LANES), F32))
        return jnp.sum(cnt, axis=1, keepdims=True)

    def value_bit(bi, t):
        cand = t + lax.shift_left(jnp.int32(1), 31 - bi)
        cf = _key_to_float(cand)
        cnt = count(lambda s, idx: s >= cf)
        return jnp.where(cnt >= topk, cand, t)

    t = lax.fori_loop(0, 32, value_bit, jnp.full((rows, LANES), INT_MIN, I32))
    thr = jnp.where(t < KEY_NEG_FLT_MAX, -FLT_MAX, _key_to_float(t))

    c_gt = count(lambda s, idx: s > thr)
    c_ge = count(lambda s, idx: s >= thr)
    need = topk - c_gt

    @pl.when(jnp.max(c_ge) > topk)
    def _():
        def index_bit(bi, x):
            cand = x + lax.shift_left(jnp.int32(1), idx_bits - 1 - bi)
            cnt = count(lambda s, idx: jnp.where(s == thr, idx, cand) < cand)
            return jnp.where(cnt < need, cand, x)

        x = lax.fori_loop(0, idx_bits, index_bit, jnp.zeros((rows, LANES), I32))

        def fix(t, c):
            s = s_ref[t]
            for j in range(nslab):
                sj = s[:, j * LANES:(j + 1) * LANES]
                idx = t * width + j * LANES + lane
                drop = jnp.where(sj == thr, idx, x) > x
                s_ref[t, :, j * LANES:(j + 1) * LANES] = jnp.where(drop, -jnp.inf, sj)
            return c

        lax.fori_loop(0, ntiles, fix, 0)

    return thr


def _pattn_kernel(qg_ref, iq_ref, iw_ref, ikt_ref, kt_ref, v_ref, o_ref,
                  s_ref, wb_ref, m_ref, l_ref, acc_ref, *, topk, idx_bits):
    bq = o_ref.shape[0]
    lb = s_ref.shape[2]
    i = pl.program_id(0)
    nkb = i + 1

    for h in range(IDX_HEADS):
        wb_ref[h] = jnp.broadcast_to(iw_ref[:, h:h + 1], (bq, LANES))

    row = lax.broadcasted_iota(I32, (bq, lb), 0)
    col = lax.broadcasted_iota(I32, (bq, lb), 1)

    def scores(kb, c):
        d = jnp.dot(iq_ref[...], ikt_ref[kb], preferred_element_type=F32)
        acc = jnp.zeros((bq, lb), F32)
        for h in range(IDX_HEADS):
            acc = acc + _tile_lanes(wb_ref[h], lb) * jnp.maximum(d[h * bq:(h + 1) * bq], 0.0)
        s_ref[kb] = jnp.where(kb * lb + col <= i * bq + row, acc, -jnp.inf)
        return c

    lax.fori_loop(0, nkb, scores, 0)

    thr = _select_topk(s_ref, nkb, topk, idx_bits)
    thr_full = _tile_lanes(thr, lb)

    m_ref[...] = jnp.full(m_ref.shape, NEG, F32)
    l_ref[...] = jnp.zeros(l_ref.shape, F32)
    acc_ref[...] = jnp.zeros(acc_ref.shape, F32)

    def attend(kb, c):
        bias = jnp.where(s_ref[kb] >= thr_full, 0.0, NEG)
        bias2 = jnp.concatenate([bias] * Q_PER_KV, axis=0)
        for kvh in range(KV_HEADS):
            s = jnp.dot(qg_ref[kvh], kt_ref[kb, kvh * HEAD_DIM:(kvh + 1) * HEAD_DIM, :],
                        preferred_element_type=F32) + bias2
            m_prev = m_ref[kvh]
            m_new = jnp.maximum(m_prev, jnp.max(s, axis=1, keepdims=True))
            alpha = jnp.exp(m_prev - m_new)
            p = jnp.exp(s - _tile_lanes(m_new, lb))
            psum = p[:, :LANES]
            for j in range(1, lb // LANES):
                psum = psum + p[:, j * LANES:(j + 1) * LANES]
            l_ref[kvh] = alpha * l_ref[kvh] + psum
            pair = kvh // 2
            pv = jnp.dot(p.astype(BF16), v_ref[kb, :, pair * LANES:(pair + 1) * LANES],
                         preferred_element_type=F32)
            acc_ref[kvh] = alpha * acc_ref[kvh] + pv
            m_ref[kvh] = m_new
        return c

    lax.fori_loop(0, nkb, attend, 0)

    for kvh in range(KV_HEADS):
        denom = jnp.sum(l_ref[kvh], axis=1, keepdims=True)
        off = (kvh % 2) * HEAD_DIM
        o = acc_ref[kvh][:, off:off + HEAD_DIM] / denom
        for g in range(Q_PER_KV):
            h = kvh * Q_PER_KV + g
            o_ref[:, h * HEAD_DIM:(h + 1) * HEAD_DIM] = o[g * bq:(g + 1) * bq]


def _pattn(qg, iqs, iw, ikt, kt, vv, topk):
    nqb, _, rows2, _ = qg.shape
    bq = rows2 // Q_PER_KV
    nkb, _, lb = ikt.shape
    assert bq == lb and nqb == nkb
    tq = nqb * bq
    idx_bits = int(nkb * lb).bit_length()
    resident = lambda shape: pl.BlockSpec(shape, lambda i: (0,) * len(shape), pipeline_mode=pl.Buffered(1))
    return pl.pallas_call(
        functools.partial(_pattn_kernel, topk=topk, idx_bits=idx_bits),
        grid=(nqb,),
        in_specs=[
            pl.BlockSpec((None, KV_HEADS, rows2, HEAD_DIM), lambda i: (i, 0, 0, 0)),
            pl.BlockSpec((None, IDX_HEADS * bq, IDX_DIM), lambda i: (i, 0, 0)),
            pl.BlockSpec((bq, IDX_HEADS), lambda i: (i, 0)),
            resident(ikt.shape), resident(kt.shape), resident(vv.shape),
        ],
        out_specs=pl.BlockSpec((bq, N_HEADS * HEAD_DIM), lambda i: (i, 0)),
        out_shape=jax.ShapeDtypeStruct((tq, N_HEADS * HEAD_DIM), F32),
        scratch_shapes=[
            pltpu.VMEM((nkb, bq, lb), F32),
            pltpu.VMEM((IDX_HEADS, bq, LANES), F32),
            pltpu.VMEM((KV_HEADS, rows2, LANES), F32),
            pltpu.VMEM((KV_HEADS, rows2, LANES), F32),
            pltpu.VMEM((KV_HEADS, rows2, LANES), F32),
        ],
        compiler_params=pltpu.CompilerParams(dimension_semantics=("arbitrary",), vmem_limit_bytes=VMEM_LIMIT),
        name="pattn",
    )(qg, iqs, iw, ikt, kt, vv)


def _sattn_kernel(pt_ref, *refs, pages, topk, idx_bits, s_new):
    del pt_ref
    np_ = PAGES_PER_STEP
    cik = refs[0:np_]
    ck = refs[np_:2 * np_]
    cv = refs[2 * np_:3 * np_]
    iq_ref, iw_ref, qbd_ref, nik_ref, nk_ref, nv_ref, o_ref, s_ref, lg_ref, vst_ref = refs[3 * np_:]
    jj = pl.program_id(1)
    rows = s_ref.shape[1]
    reps = rows // s_new

    def do_page(t, ikp, kp, vp, is_new):
        d = lax.dot_general(iq_ref[...], ikp.astype(BF16), (((1,), (1,)), ((), ())),
                            preferred_element_type=F32)
        r = jnp.maximum(d, 0.0) * iw_ref[...]
        per_q = [jnp.sum(r[q * IDX_HEADS:(q + 1) * IDX_HEADS], axis=0, keepdims=True) for q in range(s_new)]
        sc = jnp.concatenate(per_q * reps, axis=0)
        if is_new:
            qi = lax.broadcasted_iota(I32, sc.shape, 0) & (s_new - 1)
            ki = lax.broadcasted_iota(I32, sc.shape, 1)
            sc = jnp.where(ki <= qi, sc, -jnp.inf)
        s_ref[t] = sc
        lg_ref[t] = lax.dot_general(qbd_ref[...], kp.astype(BF16), (((1,), (1,)), ((), ())),
                                    preferred_element_type=F32)
        vst_ref[t] = vp.astype(BF16)

    for p in range(np_):
        do_page(jj * np_ + p, cik[p][...], ck[p][...], cv[p][...], False)

    @pl.when(jj == pl.num_programs(1) - 1)
    def _():
        do_page(pages, nik_ref[...], nk_ref[...], nv_ref[...], True)
        ntiles = pages + 1
        thr = _select_topk(s_ref, ntiles, topk, idx_bits)
        nvreg = lg_ref.shape[1] // rows

        def masked(t):
            bias = jnp.where(s_ref[t] >= thr, 0.0, NEG)
            return lg_ref[t] + jnp.concatenate([bias] * nvreg, axis=0)

        def max_body(t, m):
            return jnp.maximum(m, masked(t))
        m = lax.fori_loop(0, ntiles, max_body, jnp.full(lg_ref.shape[1:], NEG, F32))
        m = jnp.max(m, axis=1, keepdims=True)

        def pv_body(t, carry):
            l, acc = carry
            p = jnp.exp(masked(t) - m)
            acc = acc + jnp.dot(p.astype(BF16), vst_ref[t], preferred_element_type=F32)
            return l + p, acc
        l, acc = lax.fori_loop(0, ntiles, pv_body,
                               (jnp.zeros(lg_ref.shape[1:], F32), jnp.zeros(o_ref.shape, F32)))
        o_ref[...] = acc / jnp.sum(l, axis=1, keepdims=True)


def _sattn(page_table, cache_ik, cache_k, cache_v, iq_s, iw_rep, qbd, nik, nk, nv, topk):
    db, pages = page_table.shape
    page = cache_ik.shape[1]
    assert page == LANES and pages % PAGES_PER_STEP == 0
    s_new = iq_s.shape[1] // IDX_HEADS
    rows = 8
    assert rows % s_new == 0
    nrow = qbd.shape[1]
    dkv = cache_k.shape[2]
    idx_bits = int((pages + 1) * page).bit_length()

    def page_spec(width, p):
        return pl.BlockSpec((None, page, width),
                            lambda b, jj, pt, p=p: (pt[b * pages + jj * PAGES_PER_STEP + p], 0, 0))

    per_seq = lambda shape: pl.BlockSpec((None,) + shape, lambda b, jj, pt: (b, 0, 0))
    in_specs = ([page_spec(IDX_DIM, p) for p in range(PAGES_PER_STEP)]
                + [page_spec(dkv, p) for p in range(PAGES_PER_STEP)]
                + [page_spec(dkv, p) for p in range(PAGES_PER_STEP)]
                + [per_seq(iq_s.shape[1:]), per_seq(iw_rep.shape[1:]), per_seq(qbd.shape[1:]),
                   per_seq(nik.shape[1:]), per_seq(nk.shape[1:]), per_seq(nv.shape[1:])])
    grid_spec = pltpu.PrefetchScalarGridSpec(
        num_scalar_prefetch=1,
        grid=(db, pages // PAGES_PER_STEP),
        in_specs=in_specs,
        out_specs=pl.BlockSpec((None, nrow, dkv), lambda b, jj, pt: (b, 0, 0)),
        scratch_shapes=[
            pltpu.VMEM((pages + 1, rows, page), F32),
            pltpu.VMEM((pages + 1, nrow, page), F32),
            pltpu.VMEM((pages + 1, page, dkv), BF16),
        ],
    )
    args = ([cache_ik] * PAGES_PER_STEP + [cache_k] * PAGES_PER_STEP + [cache_v] * PAGES_PER_STEP
            + [iq_s, iw_rep, qbd, nik, nk, nv])
    return pl.pallas_call(
        functools.partial(_sattn_kernel, pages=pages, topk=topk, idx_bits=idx_bits, s_new=s_new),
        grid_spec=grid_spec,
        out_shape=jax.ShapeDtypeStruct((db, nrow, dkv), F32),
        compiler_params=pltpu.CompilerParams(dimension_semantics=("arbitrary", "arbitrary"),
                                             vmem_limit_bytes=VMEM_LIMIT),
        name="sattn",
    )(page_table.reshape(-1), *args)


def _merge_kernel(x_ref, attn_ref, z_ref, z1_ref, z2_ref, cb_ref, ga_ref, gb_ref,
                  wc_ref, wa_ref, wb_ref, wo_ref, g_ref, b_ref, wr_ref, br_ref,
                  h_ref, comb_ref, *, alpha):
    wc = wc_ref[...]
    y = wc[0:1] * z2_ref[...] + wc[1:2] * z1_ref[...] + wc[2:3] * z_ref[...]
    conv_o = cb_ref[...] * y
    a = jnp.dot(attn_ref[...].astype(BF16), wa_ref[...], preferred_element_type=F32)
    b = jnp.dot(conv_o.astype(BF16), wb_ref[...], preferred_element_type=F32)
    sig = lambda u: 1.0 / (1.0 + jnp.exp(-u))
    mixed = sig(ga_ref[...]) * a + sig(gb_ref[...]) * b
    mix = jnp.dot(mixed.astype(BF16), wo_ref[...], preferred_element_type=F32)
    h = _layer_norm(alpha * x_ref[...] + mix, g_ref[...], b_ref[...])
    h_ref[...] = h

    logits = jnp.dot(h, wr_ref[...], preferred_element_type=F32, precision=lax.Precision.HIGHEST) + br_ref[...]
    lane_i = lax.broadcasted_iota(I32, logits.shape, 1)
    lane = lane_i.astype(F32)
    first_where = lambda cond: jnp.min(jnp.where(cond, lane, float(LANES)), axis=1, keepdims=True)
    is_g = lane_i < N_GROUPS
    gl = jnp.where(is_g, logits, -jnp.inf)
    gmax = jnp.max(gl, axis=1, keepdims=True)
    g_sel = first_where(gl == gmax)
    g_p = 1.0 / jnp.sum(jnp.where(is_g, jnp.exp(logits - gmax), 0.0), axis=1, keepdims=True)
    grp = jnp.where((lane_i >= N_GROUPS) & (lane_i < N_GROUPS + N_EXPERTS),
                    lax.shift_right_arithmetic(lane_i - N_GROUPS, jnp.int32(2)), -1).astype(F32)
    in_grp = grp == g_sel
    e1 = jnp.where(in_grp, logits, -jnp.inf)
    max1 = jnp.max(e1, axis=1, keepdims=True)
    i1 = first_where(e1 == max1)
    e2 = jnp.where(lane == i1, -jnp.inf, e1)
    max2 = jnp.max(e2, axis=1, keepdims=True)
    i2 = first_where(e2 == max2)
    den = jnp.sum(jnp.where(in_grp, jnp.exp(logits - max1), 0.0), axis=1, keepdims=True)
    p1 = 1.0 / den
    p2 = jnp.exp(max2 - max1) / den
    tot = p1 + p2
    comb_ref[...] = jnp.where(lane == i1, p1 / tot * g_p, 0.0) + jnp.where(lane == i2, p2 / tot * g_p, 0.0)


def _merge(x, attn_o, z, z1, z2, cb, ga, gb, wc, wa, wb, wo, g, b, wr, br, alpha):
    tp, d = x.shape
    row = lambda width: pl.BlockSpec((TM, width), lambda i: (i, 0))
    const = lambda a: pl.BlockSpec(a.shape, lambda i: (0, 0), pipeline_mode=pl.Buffered(1))
    dc = z.shape[1]
    return pl.pallas_call(
        functools.partial(_merge_kernel, alpha=alpha),
        grid=(tp // TM,),
        in_specs=[row(d), row(attn_o.shape[1]), row(dc), row(dc), row(dc), row(dc), row(d), row(d),
                  const(wc), const(wa), const(wb), const(wo), const(g), const(b), const(wr), const(br)],
        out_specs=[row(d), row(LANES)],
        out_shape=[jax.ShapeDtypeStruct((tp, d), F32), jax.ShapeDtypeStruct((tp, LANES), F32)],
        compiler_params=pltpu.CompilerParams(dimension_semantics=("arbitrary",), vmem_limit_bytes=VMEM_LIMIT),
        name="merge",
    )(x, attn_o, z, z1, z2, cb, ga, gb, wc, wa, wb, wo, g, b, wr, br)


def _moe_kernel(h_ref, comb_ref, wgu_ref, wd_ref, g_ref, b_ref, o_ref, acc_ref, hb_ref, *, alpha):
    e = pl.program_id(1)
    de = wd_ref.shape[0]

    @pl.when(e == 0)
    def _():
        acc_ref[...] = jnp.zeros(acc_ref.shape, F32)
        hb_ref[...] = h_ref[...].astype(BF16)

    gu = jnp.dot(hb_ref[...], wgu_ref[...], preferred_element_type=F32)
    hg = gu[:, :de]
    hu = gu[:, de:]
    lane = lax.broadcasted_iota(I32, comb_ref.shape, 1)
    c = jnp.sum(jnp.where(lane == e + N_GROUPS, comb_ref[...], 0.0), axis=1, keepdims=True)
    act = hg * (1.0 / (1.0 + jnp.exp(-hg))) * hu * c
    acc_ref[...] += jnp.dot(act.astype(BF16), wd_ref[...], preferred_element_type=F32)

    @pl.when(e == pl.num_programs(1) - 1)
    def _():
        o_ref[...] = _layer_norm(alpha * h_ref[...] + acc_ref[...], g_ref[...], b_ref[...])


def _moe(h, comb, wgu, wd, g, b, alpha):
    tp, d = h.shape
    ne, _, de2 = wgu.shape
    de = wd.shape[1]
    return pl.pallas_call(
        functools.partial(_moe_kernel, alpha=alpha),
        grid=(tp // TM_MOE, ne),
        in_specs=[
            pl.BlockSpec((TM_MOE, d), lambda i, e: (i, 0)),
            pl.BlockSpec((TM_MOE, LANES), lambda i, e: (i, 0)),
            pl.BlockSpec((None, d, de2), lambda i, e: (e, 0, 0)),
            pl.BlockSpec((None, de, d), lambda i, e: (e, 0, 0)),
            pl.BlockSpec((1, d), lambda i, e: (0, 0)),
            pl.BlockSpec((1, d), lambda i, e: (0, 0)),
        ],
        out_specs=pl.BlockSpec((TM_MOE, d), lambda i, e: (i, 0)),
        out_shape=jax.ShapeDtypeStruct((tp, d), F32),
        scratch_shapes=[pltpu.VMEM((TM_MOE, d), F32), pltpu.VMEM((TM_MOE, d), BF16)],
        compiler_params=pltpu.CompilerParams(dimension_semantics=("arbitrary", "arbitrary"),
                                             vmem_limit_bytes=VMEM_LIMIT),
        name="moe",
    )(h, comb, wgu, wd, g, b)


def _rope_tables(pos):
    half = HEAD_DIM // 2
    inv = jnp.power(jnp.float32(ROPE_THETA), -jnp.arange(half, dtype=F32) * 2.0 / HEAD_DIM)
    ang = pos.astype(F32)[:, None] * inv[None, :]
    cos, sin = jnp.cos(ang), jnp.sin(ang)
    reps = LANES // HEAD_DIM
    return (jnp.concatenate([cos, cos] * reps, axis=1), jnp.concatenate([-sin, sin] * reps, axis=1))


def _pad_rows(a, rows):
    return jnp.pad(a, [(0, rows - a.shape[0])] + [(0, 0)] * (a.ndim - 1))


def kernel(x_prompt, x_sample, cache_k, cache_v, cache_idx_k, state_conv, page_table, meta_tokens, w_in, b_in, w_conv, w_attn_up, w_conv_out, w_o, ln1_g, ln1_b, w_group, b_group, w_expert_router, b_expert_router, w_gate, w_up, w_down, ln2_g, ln2_b):
    bsz, s_p, d = x_prompt.shape
    db, s_s, _ = x_sample.shape
    depth = w_in.shape[0]
    assert bsz == 1, "prompt group is served one sequence at a time"
    assert s_s >= CONV_WIDTH - 1
    page = cache_k.shape[2]
    pages = page_table.shape[1]
    past_len = pages * page
    t_p = s_p + N_META
    t_s = db * s_s
    t_all = t_p + t_s
    topk_p = min(TOPK_MAX, t_p // 4)
    topk_s = min(TOPK_MAX, (past_len + s_s) // 4)
    alpha = (2 * depth) ** 0.25

    d_attn = N_HEADS * HEAD_DIM
    d_kv = KV_HEADS * HEAD_DIM
    d_idx = IDX_HEADS * IDX_DIM
    d_conv = w_conv.shape[2]
    d_exp = w_gate.shape[3]
    segs, offs, n_proj = _proj_layout(d_attn, d_kv, d_idx, d_conv, d)
    widths = {name: width for name, width, _ in segs}
    ref_order = ["q", "k", "v", "iq", "iw", "ik", "cu", "cb", "cc", "ga", "gb"]
    ref_starts = np.concatenate([[0], np.cumsum([widths[nm] for nm in ref_order])])

    tile = max(TM, TM_MOE)
    tp = -(-t_all // tile) * tile
    tq = -(-t_p // BQ) * BQ
    assert tq <= tp
    nqb = tq // BQ

    pos = jnp.concatenate([jnp.arange(t_p, dtype=jnp.int32),
                           jnp.tile(past_len + jnp.arange(s_s, dtype=jnp.int32), db),
                           jnp.zeros((tp - t_all,), jnp.int32)])
    cos_t, sin_t = _rope_tables(pos)

    h_p = jnp.concatenate([meta_tokens.astype(x_prompt.dtype), x_prompt[0]], axis=0)
    h = _pad_rows(jnp.concatenate([h_p, x_sample.reshape(t_s, d)], axis=0), tp)

    outs = {k: [] for k in ("kp", "vp", "ikp", "cp", "ks", "vs", "iks", "cs")}
    for l in range(depth):
        pieces_w, pieces_b = [], []
        for nm, width, padded in segs:
            a = int(ref_starts[ref_order.index(nm)])
            pieces_w.append(jnp.pad(w_in[l][:, a:a + width], [(0, 0), (0, padded - width)]))
            pieces_b.append(jnp.pad(b_in[l][a:a + width], [(0, padded - width)]))
        w_p = jnp.concatenate(pieces_w, axis=1).astype(BF16)
        b_p = jnp.concatenate(pieces_b)[None, :]

        q_r, k_r, v_r, iq_r, ik_r, iw_r, z, cb, ga, gb = _proj(h, w_p, b_p, cos_t, sin_t, offs, widths)

        scale = HEAD_DIM ** -0.5
        qg = (q_r[:tq] * scale).astype(BF16).reshape(nqb, BQ, KV_HEADS, Q_PER_KV, HEAD_DIM)
        qg = qg.transpose(0, 2, 3, 1, 4).reshape(nqb, KV_HEADS, Q_PER_KV * BQ, HEAD_DIM)
        iqs = iq_r[:tq].astype(BF16).reshape(nqb, BQ, IDX_HEADS, IDX_DIM)
        iqs = iqs.transpose(0, 2, 1, 3).reshape(nqb, IDX_HEADS * BQ, IDX_DIM)
        ikt = ik_r[:tq].astype(BF16).reshape(nqb, BQ, IDX_DIM).transpose(0, 2, 1)
        kt = k_r[:tq].astype(BF16).reshape(nqb, BQ, d_kv).transpose(0, 2, 1)
        vv = v_r[:tq].astype(BF16).reshape(nqb, BQ, d_kv)
        attn_p = _pattn(qg, iqs, iw_r[:tq], ikt, kt, vv, topk_p)

        sl = slice(t_p, t_all)
        iq_s = iq_r[sl].astype(BF16).reshape(db, s_s * IDX_HEADS, IDX_DIM)
        iw_rep = jnp.broadcast_to(iw_r[sl].reshape(db, s_s * IDX_HEADS, 1), (db, s_s * IDX_HEADS, LANES))
        q_s = (q_r[sl] * scale).astype(BF16).reshape(db, s_s, KV_HEADS, Q_PER_KV, HEAD_DIM)
        q_s = q_s.transpose(0, 2, 3, 1, 4)
        eye = jnp.eye(KV_HEADS, dtype=BF16)
        qbd = (q_s[:, :, :, :, None, :] * eye[None, :, None, None, :, None])
        qbd = qbd.reshape(db, KV_HEADS * Q_PER_KV * s_s, d_kv)
        new_page = lambda a: jnp.pad(a[sl].reshape(db, s_s, a.shape[1]), [(0, 0), (0, page - s_s), (0, 0)])
        r_s = _sattn(page_table, cache_idx_k[l], cache_k[l].reshape(-1, page, d_kv),
                     cache_v[l].reshape(-1, page, d_kv), iq_s, iw_rep, qbd,
                     new_page(ik_r), new_page(k_r), new_page(v_r), topk_s)
        r_s = r_s.reshape(db, KV_HEADS, Q_PER_KV, s_s, KV_HEADS, HEAD_DIM)
        attn_s = jnp.stack([r_s[:, kvh, :, :, kvh, :] for kvh in range(KV_HEADS)], axis=1)
        attn_s = attn_s.transpose(0, 3, 1, 2, 4).reshape(t_s, d_attn)

        attn_o = _pad_rows(jnp.concatenate([attn_p[:t_p], attn_s], axis=0), tp)

        zp = jnp.concatenate([jnp.zeros((CONV_WIDTH - 1, d_conv), F32), z[:t_p]], axis=0)
        zs = jnp.concatenate([state_conv[l].astype(F32), z[sl].reshape(db, s_s, d_conv)], axis=1)
        z1 = _pad_rows(jnp.concatenate([zp[1:1 + t_p], zs[:, 1:1 + s_s].reshape(t_s, d_conv)], axis=0), tp)
        z2 = _pad_rows(jnp.concatenate([zp[0:t_p], zs[:, 0:s_s].reshape(t_s, d_conv)], axis=0), tp)

        wr = jnp.pad(jnp.concatenate([w_group[l], w_expert_router[l]], axis=1),
                     [(0, 0), (0, LANES - N_GROUPS - N_EXPERTS)])
        br = jnp.pad(jnp.concatenate([b_group[l], b_expert_router[l]]), [(0, LANES - N_GROUPS - N_EXPERTS)])[None]
        h1, comb = _merge(h, attn_o, z, z1, z2, cb, ga, gb, w_conv[l],
                          w_attn_up[l].astype(BF16), w_conv_out[l].astype(BF16), w_o[l].astype(BF16),
                          ln1_g[l][None], ln1_b[l][None], wr, br, alpha)

        wgu = jnp.concatenate([w_gate[l], w_up[l]], axis=2).astype(BF16)
        h = _moe(h1, comb, wgu, w_down[l].astype(BF16), ln2_g[l][None], ln2_b[l][None], alpha)

        outs["kp"].append(k_r[:t_p].reshape(bsz, t_p, KV_HEADS, HEAD_DIM))
        outs["vp"].append(v_r[:t_p].reshape(bsz, t_p, KV_HEADS, HEAD_DIM))
        outs["ikp"].append(ik_r[:t_p].reshape(bsz, t_p, IDX_DIM))
        outs["cp"].append(zp[-(CONV_WIDTH - 1):].reshape(bsz, CONV_WIDTH - 1, d_conv))
        outs["ks"].append(k_r[sl].reshape(db, s_s, KV_HEADS, HEAD_DIM))
        outs["vs"].append(v_r[sl].reshape(db, s_s, KV_HEADS, HEAD_DIM))
        outs["iks"].append(ik_r[sl].reshape(db, s_s, IDX_DIM))
        outs["cs"].append(zs[:, -(CONV_WIDTH - 1):])

    y_prompt = h[N_META:t_p].reshape(bsz, s_p, d)
    y_sample = h[t_p:t_all].reshape(db, s_s, d)
    st = lambda k: jnp.stack(outs[k])
    return (y_prompt, y_sample, st("kp"), st("vp"), st("ikp"), st("cp"), st("ks"), st("vs"), st("iks"), st("cs"))
```

```python
import functools
import math

import numpy as np
import jax
import jax.numpy as jnp
from jax import lax
from jax.experimental import pallas as pl
from jax.experimental.pallas import tpu as pltpu

F32 = jnp.float32
BF16 = jnp.bfloat16
I32 = jnp.int32

N_META = 16
N_HEADS = 8
HEAD_DIM = 64
KV_HEADS = 4
Q_PER_KV = N_HEADS // KV_HEADS
IDX_HEADS = 8
IDX_DIM = 64
TOPK_MAX = 256
CONV_WIDTH = 3
N_GROUPS = 4
EXPERTS_PER_GROUP = 4
N_EXPERTS = N_GROUPS * EXPERTS_PER_GROUP
ROPE_THETA = 10000.0
LN_EPS = 1e-5
IDX_W_SCALE = (IDX_HEADS ** -0.5) * (IDX_DIM ** -0.5)

LANES = 128
SUBLANES = 8
INT_MIN = -(2 ** 31)
KEY_NEG_FLT_MAX = INT_MIN + (1 << 23)
FLT_MAX = float(np.finfo(np.float32).max)
NEG = -1e30
VMEM_LIMIT = 56 * 1024 * 1024

TM = 256
TM_MOE = 512
BQ = 256
LB = 512
ROW_GROUP = 128
PAGES_PER_STEP = 8


def _tile_lanes(x, n):
    reps = n // x.shape[1]
    return x if reps == 1 else jnp.concatenate([x] * reps, axis=1)


def _layer_norm(x, g, b):
    mu = jnp.mean(x, axis=-1, keepdims=True)
    xc = x - mu
    var = jnp.mean(xc * xc, axis=-1, keepdims=True)
    return xc * lax.rsqrt(var + LN_EPS) * g + b


def _proj_layout(d_attn, d_kv, d_idx, d_conv, d_model):
    segs = [("q", d_attn, d_attn), ("k", d_kv, d_kv), ("v", d_kv, d_kv), ("iq", d_idx, d_idx),
            ("iw", IDX_HEADS, LANES), ("ik", IDX_DIM, LANES),
            ("cu", d_conv, d_conv), ("cb", d_conv, d_conv), ("cc", d_conv, d_conv),
            ("ga", d_model, d_model), ("gb", d_model, d_model)]
    offs, o = {}, 0
    for name, _, padded in segs:
        offs[name] = (o, padded)
        o += padded
    return segs, offs, o


def _proj_kernel(x_ref, w_ref, b_ref, cos_ref, sin_ref,
                 q_ref, k_ref, v_ref, iq_ref, ik_ref, iw_ref, z_ref, cb_ref, ga_ref, gb_ref, *, offs):
    xb = x_ref[...].astype(BF16)

    def seg(name):
        a, n = offs[name]
        return jnp.dot(xb, w_ref[:, a:a + n], preferred_element_type=F32) + b_ref[:, a:a + n]

    cos = cos_ref[...]
    sin = sin_ref[...]

    def rope(y):
        n = y.shape[1]
        lane = lax.broadcasted_iota(I32, y.shape, 1)
        first_half = (lane & (HEAD_DIM - 1)) < (HEAD_DIM // 2)
        swapped = jnp.where(first_half, pltpu.roll(y, n - HEAD_DIM // 2, 1), pltpu.roll(y, HEAD_DIM // 2, 1))
        return y * _tile_lanes(cos, n) + swapped * _tile_lanes(sin, n)

    q_ref[...] = rope(seg("q"))
    k_ref[...] = rope(seg("k"))
    v_ref[...] = seg("v")
    iq_ref[...] = rope(seg("iq"))
    ik_ref[...] = rope(seg("ik"))[:, :IDX_DIM]
    iw_ref[...] = seg("iw")[:, :IDX_HEADS] * IDX_W_SCALE
    z_ref[...] = seg("cc") * seg("cu")
    cb_ref[...] = seg("cb")
    ga_ref[...] = seg("ga")
    gb_ref[...] = seg("gb")


def _proj(x, w, b, cos, sin, offs, widths):
    tp, d = x.shape
    n = w.shape[1]
    row = lambda width: pl.BlockSpec((TM, width), lambda i: (i, 0))
    const = lambda shape: pl.BlockSpec(shape, lambda i: (0, 0), pipeline_mode=pl.Buffered(1))
    out_w = [widths["q"], widths["k"], widths["v"], widths["iq"], IDX_DIM, IDX_HEADS,
             widths["cu"], widths["cb"], widths["ga"], widths["gb"]]
    return pl.pallas_call(
        functools.partial(_proj_kernel, offs=offs),
        grid=(tp // TM,),
        in_specs=[row(d), const((d, n)), const((1, n)), row(LANES), row(LANES)],
        out_specs=[row(wd) for wd in out_w],
        out_shape=[jax.ShapeDtypeStruct((tp, wd), F32) for wd in out_w],
        compiler_params=pltpu.CompilerParams(dimension_semantics=("arbitrary",), vmem_limit_bytes=VMEM_LIMIT),
        name="proj",
    )(x, w, b, cos, sin)


def _key_to_float(c):
    bits = jnp.where(c >= 0, c, c ^ jnp.int32(0x7FFFFFFF))
    return lax.bitcast_convert_type(bits, F32)


def _float_to_key(x):
    bits = lax.bitcast_convert_type(x, I32)
    return jnp.where(bits >= 0, bits, bits ^ jnp.int32(0x7FFFFFFF))


def _select_topk(s_ref, thr_ref, ntiles, topk, idx_bits, rg):
    _, rows, width = s_ref.shape
    nslab = width // LANES
    nbin = -(-topk // LANES)
    assert nbin <= nslab and rows % rg == 0
    static = isinstance(ntiles, int)

    def loop(body, init):
        if not static:
            return lax.fori_loop(0, ntiles, body, init)
        carry = init
        for t in range(ntiles):
            carry = body(t, carry)
        return carry

    for r0 in range(0, rows, rg):
        lane = lax.broadcasted_iota(I32, (rg, LANES), 1)
        slab = lambda t, j: s_ref[t, r0:r0 + rg, j * LANES:(j + 1) * LANES]

        def count(pred):
            def body(t, cnt):
                for j in range(nslab):
                    cnt = cnt + jnp.where(pred(slab(t, j), t * width + j * LANES + lane), 1.0, 0.0)
                return cnt
            return jnp.sum(loop(body, jnp.zeros((rg, LANES), F32)), axis=1, keepdims=True)

        def bin_body(t, bm):
            bm = list(bm)
            for j in range(nslab):
                bm[j % nbin] = jnp.maximum(bm[j % nbin], slab(t, j))
            return tuple(bm)

        bm = loop(bin_body, tuple(jnp.full((rg, LANES), -jnp.inf, F32) for _ in range(nbin)))
        lo, hi = bm[0], bm[0]
        for j in range(1, nbin):
            lo, hi = jnp.minimum(lo, bm[j]), jnp.maximum(hi, bm[j])
        k_lo = _float_to_key(jnp.min(lo, axis=1, keepdims=True))
        k_hi = _float_to_key(jnp.max(hi, axis=1, keepdims=True))
        nb_rows = (32 - lax.clz(k_lo ^ k_hi)).astype(F32)
        nbits = jnp.max(nb_rows).astype(I32)
        himask = jnp.where(nbits >= 32, jnp.int32(0), lax.shift_left(jnp.int32(-1), jnp.minimum(nbits, 31)))
        t0 = jnp.broadcast_to(((k_hi ^ jnp.int32(INT_MIN)) & himask) ^ jnp.int32(INT_MIN), (rg, LANES))

        def cond(st):
            b, _, done = st
            return jnp.logical_and(b >= 0, jnp.min(done) < 0.5)

        def body(st):
            b, t, done = st
            cand = t + lax.shift_left(jnp.int32(1), b)
            cf = _key_to_float(cand)
            cnt = count(lambda s, idx: s >= cf)
            t = jnp.where(cnt >= topk, jnp.where(done < 0.5, cand, t), t)
            done = jnp.where(cnt == topk, 1.0, done)
            return b - 1, t, done

        _, t, done = lax.while_loop(cond, body, (nbits - 1, t0, jnp.zeros((rg, LANES), F32)))
        thr = jnp.where(t < KEY_NEG_FLT_MAX, -FLT_MAX, _key_to_float(t))
        thr_ref[r0:r0 + rg, :] = thr

        @pl.when(jnp.min(done) < 0.5)
        def _():
            c_gt = count(lambda s, idx: s > thr)
            c_ge = count(lambda s, idx: s >= thr)
            need = topk - c_gt

            @pl.when(jnp.max(c_ge) > topk)
            def _():
                def index_bit(bi, x):
                    cand = x + lax.shift_left(jnp.int32(1), idx_bits - 1 - bi)
                    cnt = count(lambda s, idx: jnp.where(s == thr, idx, cand) < cand)
                    return jnp.where(cnt < need, cand, x)

                x = lax.fori_loop(0, idx_bits, index_bit, jnp.zeros((rg, LANES), I32))

                def fix(t_, c):
                    for j in range(nslab):
                        sj = slab(t_, j)
                        drop = jnp.where(sj == thr, t_ * width + j * LANES + lane, x) > x
                        s_ref[t_, r0:r0 + rg, j * LANES:(j + 1) * LANES] = jnp.where(drop, -jnp.inf, sj)
                    return c

                loop(fix, 0)


def _pattn_kernel(qg_ref, iq_ref, iw_ref, ikt_ref, kt_ref, v_ref, o_ref,
                  s_ref, thr_ref, wb_ref, m_ref, l_ref, acc_ref, *, topk, idx_bits):
    bq = o_ref.shape[0]
    lb = s_ref.shape[2]
    i = pl.program_id(0)
    nkb = pl.cdiv((i + 1) * bq, lb)

    for h in range(IDX_HEADS):
        wb_ref[h] = jnp.broadcast_to(iw_ref[:, h:h + 1], (bq, LANES))

    row = lax.broadcasted_iota(I32, (bq, lb), 0)
    col = lax.broadcasted_iota(I32, (bq, lb), 1)

    def scores(kb, c):
        d = jnp.dot(iq_ref[...], ikt_ref[kb], preferred_element_type=F32)
        acc = jnp.zeros((bq, lb), F32)
        for h in range(IDX_HEADS):
            acc = acc + _tile_lanes(wb_ref[h], lb) * jnp.maximum(d[h * bq:(h + 1) * bq], 0.0)
        s_ref[kb] = jnp.where(kb * lb + col <= i * bq + row, acc, -jnp.inf)
        return c

    lax.fori_loop(0, nkb, scores, 0)

    _select_topk(s_ref, thr_ref, nkb, topk, idx_bits, ROW_GROUP)
    thr_full = _tile_lanes(thr_ref[...], lb)

    m_ref[...] = jnp.full(m_ref.shape, NEG, F32)
    l_ref[...] = jnp.zeros(l_ref.shape, F32)
    acc_ref[...] = jnp.zeros(acc_ref.shape, F32)

    def attend(kb, c):
        bias = jnp.where(s_ref[kb] >= thr_full, 0.0, NEG)
        bias2 = jnp.concatenate([bias] * Q_PER_KV, axis=0)
        for kvh in range(KV_HEADS):
            s = jnp.dot(qg_ref[kvh], kt_ref[kb, kvh * HEAD_DIM:(kvh + 1) * HEAD_DIM, :],
                        preferred_element_type=F32) + bias2
            m_prev = m_ref[kvh]
            m_new = jnp.maximum(m_prev, jnp.max(s, axis=1, keepdims=True))
            alpha = jnp.exp2(m_prev - m_new)
            p = jnp.exp2(s - _tile_lanes(m_new, lb))
            psum = p[:, :LANES]
            for j in range(1, lb // LANES):
                psum = psum + p[:, j * LANES:(j + 1) * LANES]
            l_ref[kvh] = alpha * l_ref[kvh] + psum
            pair = kvh // 2
            pv = jnp.dot(p.astype(BF16), v_ref[kb, :, pair * LANES:(pair + 1) * LANES],
                         preferred_element_type=F32)
            acc_ref[kvh] = alpha * acc_ref[kvh] + pv
            m_ref[kvh] = m_new
        return c

    lax.fori_loop(0, nkb, attend, 0)

    for kvh in range(KV_HEADS):
        denom = jnp.sum(l_ref[kvh], axis=1, keepdims=True)
        off = (kvh % 2) * HEAD_DIM
        o = acc_ref[kvh][:, off:off + HEAD_DIM] / denom
        for g in range(Q_PER_KV):
            h = kvh * Q_PER_KV + g
            o_ref[:, h * HEAD_DIM:(h + 1) * HEAD_DIM] = o[g * bq:(g + 1) * bq]


def _pattn(qg, iqs, iw, ikt, kt, vv, topk):
    nqb, _, rows2, _ = qg.shape
    bq = rows2 // Q_PER_KV
    nkb, _, lb = ikt.shape
    assert nkb * lb >= nqb * bq
    tq = nqb * bq
    idx_bits = int(nkb * lb).bit_length()
    resident = lambda shape: pl.BlockSpec(shape, lambda i: (0,) * len(shape), pipeline_mode=pl.Buffered(1))
    return pl.pallas_call(
        functools.partial(_pattn_kernel, topk=topk, idx_bits=idx_bits),
        grid=(nqb,),
        in_specs=[
            pl.BlockSpec((None, KV_HEADS, rows2, HEAD_DIM), lambda i: (i, 0, 0, 0)),
            pl.BlockSpec((None, IDX_HEADS * bq, IDX_DIM), lambda i: (i, 0, 0)),
            pl.BlockSpec((bq, IDX_HEADS), lambda i: (i, 0)),
            resident(ikt.shape), resident(kt.shape), resident(vv.shape),
        ],
        out_specs=pl.BlockSpec((bq, N_HEADS * HEAD_DIM), lambda i: (i, 0)),
        out_shape=jax.ShapeDtypeStruct((tq, N_HEADS * HEAD_DIM), F32),
        scratch_shapes=[
            pltpu.VMEM((nkb, bq, lb), F32),
            pltpu.VMEM((bq, LANES), F32),
            pltpu.VMEM((IDX_HEADS, bq, LANES), F32),
            pltpu.VMEM((KV_HEADS, rows2, LANES), F32),
            pltpu.VMEM((KV_HEADS, rows2, LANES), F32),
            pltpu.VMEM((KV_HEADS, rows2, LANES), F32),
        ],
        compiler_params=pltpu.CompilerParams(dimension_semantics=("arbitrary",), vmem_limit_bytes=VMEM_LIMIT),
        name="pattn",
    )(qg, iqs, iw, ikt, kt, vv)


def _sattn_kernel(pt_ref, *refs, topk, idx_bits, s_new):
    del pt_ref
    np_ = PAGES_PER_STEP
    cik = refs[0:np_]
    ck = refs[np_:2 * np_]
    cv = refs[2 * np_:3 * np_]
    iq_ref, iw_ref, qbd_ref, nik_ref, nk_ref, nv_ref, o_ref, s_ref, thr_ref, lg_ref, vst_ref = refs[3 * np_:]
    b = pl.program_id(0)
    jj = pl.program_id(1)
    nsteps = pl.num_programs(1)
    ntiles, rows, width = s_ref.shape
    page = width // np_
    reps = rows // s_new

    @pl.when(jnp.logical_and(b == 0, jj == 0))
    def _():
        s_ref[ntiles - 1] = jnp.full((rows, width), -jnp.inf, F32)
        lg_ref[ntiles - 1] = jnp.zeros(lg_ref.shape[1:], F32)
        vst_ref[ntiles - 1] = jnp.zeros(vst_ref.shape[1:], BF16)

    def do_page(t, p, ikt, kt, vt, is_new):
        d = jnp.dot(iq_ref[...], ikt.astype(BF16), preferred_element_type=F32)
        r = jnp.maximum(d, 0.0) * iw_ref[...]
        per_q = [jnp.sum(r[q * IDX_HEADS:(q + 1) * IDX_HEADS], axis=0, keepdims=True) for q in range(s_new)]
        sc = jnp.concatenate(per_q * reps, axis=0)
        if is_new:
            qi = lax.broadcasted_iota(I32, sc.shape, 0) & (s_new - 1)
            ki = lax.broadcasted_iota(I32, sc.shape, 1)
            sc = jnp.where(ki <= qi, sc, -jnp.inf)
        s_ref[t, :, p * page:(p + 1) * page] = sc
        lg_ref[t, :, p * page:(p + 1) * page] = jnp.dot(qbd_ref[...], kt.astype(BF16),
                                                        preferred_element_type=F32)
        vst_ref[t, :, p * page:(p + 1) * page] = vt.astype(BF16)

    for p in range(np_):
        do_page(jj, p, cik[p][...], ck[p][...], cv[p][...], False)

    @pl.when(jj == nsteps - 1)
    def _():
        do_page(ntiles - 1, 0, nik_ref[...], nk_ref[...], nv_ref[...], True)
        _select_topk(s_ref, thr_ref, ntiles, topk, idx_bits, rows)
        thr_full = _tile_lanes(thr_ref[...], width)
        nvreg = lg_ref.shape[1] // rows

        def masked(t):
            bias = jnp.where(s_ref[t] >= thr_full, 0.0, NEG)
            return lg_ref[t] + jnp.concatenate([bias] * nvreg, axis=0)

        m = jnp.max(masked(0), axis=1, keepdims=True)
        for t in range(1, ntiles):
            m = jnp.maximum(m, jnp.max(masked(t), axis=1, keepdims=True))
        l = jnp.zeros((lg_ref.shape[1], 1), F32)
        acc = jnp.zeros(o_ref.shape, F32)
        for t in range(ntiles):
            p = jnp.exp2(masked(t) - m)
            l = l + jnp.sum(p, axis=1, keepdims=True)
            acc = acc + lax.dot_general(p.astype(BF16), vst_ref[t], (((1,), (1,)), ((), ())),
                                        preferred_element_type=F32)
        o_ref[...] = acc / l


def _sattn(page_table, cache_ikt, cache_kt, cache_vt, iq_s, iw_rep, qbd, nikt, nkt, nvt, topk):
    db, pages = page_table.shape
    page = cache_ikt.shape[2]
    assert page == LANES and pages % PAGES_PER_STEP == 0
    s_new = iq_s.shape[1] // IDX_HEADS
    rows = SUBLANES
    assert rows % s_new == 0 and s_new & (s_new - 1) == 0
    nrow = qbd.shape[1]
    dkv = cache_kt.shape[1]
    nsteps = pages // PAGES_PER_STEP
    width = PAGES_PER_STEP * page
    idx_bits = int((nsteps + 1) * width).bit_length()

    def page_spec(depth, p):
        return pl.BlockSpec((None, depth, page),
                            lambda b, jj, pt, p=p: (pt[b * pages + jj * PAGES_PER_STEP + p], 0, 0))

    per_seq = lambda shape: pl.BlockSpec((None,) + shape, lambda b, jj, pt: (b, 0, 0))
    in_specs = ([page_spec(IDX_DIM, p) for p in range(PAGES_PER_STEP)]
                + [page_spec(dkv, p) for p in range(PAGES_PER_STEP)]
                + [page_spec(dkv, p) for p in range(PAGES_PER_STEP)]
                + [per_seq(iq_s.shape[1:]), per_seq(iw_rep.shape[1:]), per_seq(qbd.shape[1:]),
                   per_seq(nikt.shape[1:]), per_seq(nkt.shape[1:]), per_seq(nvt.shape[1:])])
    grid_spec = pltpu.PrefetchScalarGridSpec(
        num_scalar_prefetch=1,
        grid=(db, nsteps),
        in_specs=in_specs,
        out_specs=pl.BlockSpec((None, nrow, dkv), lambda b, jj, pt: (b, 0, 0)),
        scratch_shapes=[
            pltpu.VMEM((nsteps + 1, rows, width), F32),
            pltpu.VMEM((rows, LANES), F32),
            pltpu.VMEM((nsteps + 1, nrow, width), F32),
            pltpu.VMEM((nsteps + 1, dkv, width), BF16),
        ],
    )
    args = ([cache_ikt] * PAGES_PER_STEP + [cache_kt] * PAGES_PER_STEP + [cache_vt] * PAGES_PER_STEP
            + [iq_s, iw_rep, qbd, nikt, nkt, nvt])
    return pl.pallas_call(
        functools.partial(_sattn_kernel, topk=topk, idx_bits=idx_bits, s_new=s_new),
        grid_spec=grid_spec,
        out_shape=jax.ShapeDtypeStruct((db, nrow, dkv), F32),
        compiler_params=pltpu.CompilerParams(dimension_semantics=("arbitrary", "arbitrary"),
                                             vmem_limit_bytes=VMEM_LIMIT),
        name="sattn",
    )(page_table.reshape(-1), *args)


def _merge_kernel(x_ref, attn_ref, z_ref, z1_ref, z2_ref, cb_ref, ga_ref, gb_ref,
                  wc_ref, wa_ref, wb_ref, wo_ref, g_ref, b_ref, wr_ref, br_ref,
                  h_ref, comb_ref, *, alpha):
    wc = wc_ref[...]
    y = wc[0:1] * z2_ref[...] + wc[1:2] * z1_ref[...] + wc[2:3] * z_ref[...]
    conv_o = cb_ref[...] * y
    a = jnp.dot(attn_ref[...].astype(BF16), wa_ref[...], preferred_element_type=F32)
    b = jnp.dot(conv_o.astype(BF16), wb_ref[...], preferred_element_type=F32)
    sig = lambda u: 1.0 / (1.0 + jnp.exp(-u))
    mixed = sig(ga_ref[...]) * a + sig(gb_ref[...]) * b
    mix = jnp.dot(mixed.astype(BF16), wo_ref[...], preferred_element_type=F32)
    h = _layer_norm(alpha * x_ref[...] + mix, g_ref[...], b_ref[...])
    h_ref[...] = h

    logits = jnp.dot(h, wr_ref[...], preferred_element_type=F32, precision=lax.Precision.HIGHEST) + br_ref[...]
    lane_i = lax.broadcasted_iota(I32, logits.shape, 1)
    lane = lane_i.astype(F32)
    first_where = lambda cond: jnp.min(jnp.where(cond, lane, float(LANES)), axis=1, keepdims=True)
    is_g = lane_i < N_GROUPS
    gl = jnp.where(is_g, logits, -jnp.inf)
    gmax = jnp.max(gl, axis=1, keepdims=True)
    g_sel = first_where(gl == gmax)
    g_p = 1.0 / jnp.sum(jnp.where(is_g, jnp.exp(logits - gmax), 0.0), axis=1, keepdims=True)
    grp = jnp.where((lane_i >= N_GROUPS) & (lane_i < N_GROUPS + N_EXPERTS),
                    lax.shift_right_arithmetic(lane_i - N_GROUPS, jnp.int32(int(math.log2(EXPERTS_PER_GROUP)))),
                    -1).astype(F32)
    in_grp = grp == g_sel
    e1 = jnp.where(in_grp, logits, -jnp.inf)
    max1 = jnp.max(e1, axis=1, keepdims=True)
    i1 = first_where(e1 == max1)
    e2 = jnp.where(lane == i1, -jnp.inf, e1)
    max2 = jnp.max(e2, axis=1, keepdims=True)
    i2 = first_where(e2 == max2)
    den = jnp.sum(jnp.where(in_grp, jnp.exp(logits - max1), 0.0), axis=1, keepdims=True)
    p1 = 1.0 / den
    p2 = jnp.exp(max2 - max1) / den
    tot = p1 + p2
    comb_ref[...] = jnp.where(lane == i1, p1 / tot * g_p, 0.0) + jnp.where(lane == i2, p2 / tot * g_p, 0.0)


def _merge(x, attn_o, z, z1, z2, cb, ga, gb, wc, wa, wb, wo, g, b, wr, br, alpha):
    tp, d = x.shape
    row = lambda width: pl.BlockSpec((TM, width), lambda i: (i, 0))
    const = lambda a: pl.BlockSpec(a.shape, lambda i: (0, 0), pipeline_mode=pl.Buffered(1))
    dc = z.shape[1]
    return pl.pallas_call(
        functools.partial(_merge_kernel, alpha=alpha),
        grid=(tp // TM,),
        in_specs=[row(d), row(attn_o.shape[1]), row(dc), row(dc), row(dc), row(dc), row(d), row(d),
                  const(wc), const(wa), const(wb), const(wo), const(g), const(b), const(wr), const(br)],
        out_specs=[row(d), row(LANES)],
        out_shape=[jax.ShapeDtypeStruct((tp, d), F32), jax.ShapeDtypeStruct((tp, LANES), F32)],
        compiler_params=pltpu.CompilerParams(dimension_semantics=("arbitrary",), vmem_limit_bytes=VMEM_LIMIT),
        name="merge",
    )(x, attn_o, z, z1, z2, cb, ga, gb, wc, wa, wb, wo, g, b, wr, br)


def _moe_kernel(h_ref, comb_ref, wgu_ref, wd_ref, g_ref, b_ref, o_ref, acc_ref, hb_ref, *, alpha):
    e = pl.program_id(1)
    de = wd_ref.shape[0]

    @pl.when(e == 0)
    def _():
        acc_ref[...] = jnp.zeros(acc_ref.shape, F32)
        hb_ref[...] = h_ref[...].astype(BF16)

    gu = jnp.dot(hb_ref[...], wgu_ref[...], preferred_element_type=F32)
    hg = gu[:, :de]
    hu = gu[:, de:]
    lane = lax.broadcasted_iota(I32, comb_ref.shape, 1)
    c = jnp.sum(jnp.where(lane == e + N_GROUPS, comb_ref[...], 0.0), axis=1, keepdims=True)
    act = hg * (1.0 / (1.0 + jnp.exp(-hg))) * hu * c
    acc_ref[...] += jnp.dot(act.astype(BF16), wd_ref[...], preferred_element_type=F32)

    @pl.when(e == pl.num_programs(1) - 1)
    def _():
        o_ref[...] = _layer_norm(alpha * h_ref[...] + acc_ref[...], g_ref[...], b_ref[...])


def _moe(h, comb, wgu, wd, g, b, alpha):
    tp, d = h.shape
    ne, _, de2 = wgu.shape
    de = wd.shape[1]
    return pl.pallas_call(
        functools.partial(_moe_kernel, alpha=alpha),
        grid=(tp // TM_MOE, ne),
        in_specs=[
            pl.BlockSpec((TM_MOE, d), lambda i, e: (i, 0)),
            pl.BlockSpec((TM_MOE, LANES), lambda i, e: (i, 0)),
            pl.BlockSpec((None, d, de2), lambda i, e: (e, 0, 0)),
            pl.BlockSpec((None, de, d), lambda i, e: (e, 0, 0)),
            pl.BlockSpec((1, d), lambda i, e: (0, 0)),
            pl.BlockSpec((1, d), lambda i, e: (0, 0)),
        ],
        out_specs=pl.BlockSpec((TM_MOE, d), lambda i, e: (i, 0)),
        out_shape=jax.ShapeDtypeStruct((tp, d), F32),
        scratch_shapes=[pltpu.VMEM((TM_MOE, d), F32), pltpu.VMEM((TM_MOE, d), BF16)],
        compiler_params=pltpu.CompilerParams(dimension_semantics=("arbitrary", "arbitrary"),
                                             vmem_limit_bytes=VMEM_LIMIT),
        name="moe",
    )(h, comb, wgu, wd, g, b)


def _rope_tables(pos):
    half = HEAD_DIM // 2
    inv = jnp.power(jnp.float32(ROPE_THETA), -jnp.arange(half, dtype=F32) * 2.0 / HEAD_DIM)
    ang = pos.astype(F32)[:, None] * inv[None, :]
    cos, sin = jnp.cos(ang), jnp.sin(ang)
    reps = LANES // HEAD_DIM
    return (jnp.concatenate([cos, cos] * reps, axis=1), jnp.concatenate([-sin, sin] * reps, axis=1))


def _pad_rows(a, rows):
    return jnp.pad(a, [(0, rows - a.shape[0])] + [(0, 0)] * (a.ndim - 1))


def kernel(x_prompt, x_sample, cache_k, cache_v, cache_idx_k, state_conv, page_table, meta_tokens, w_in, b_in, w_conv, w_attn_up, w_conv_out, w_o, ln1_g, ln1_b, w_group, b_group, w_expert_router, b_expert_router, w_gate, w_up, w_down, ln2_g, ln2_b):
    bsz, s_p, d = x_prompt.shape
    db, s_s, _ = x_sample.shape
    depth = w_in.shape[0]
    assert bsz == 1, "prompt group is served one sequence at a time"
    assert s_s >= CONV_WIDTH - 1
    n_phys, page = cache_k.shape[1], cache_k.shape[2]
    pages = page_table.shape[1]
    past_len = pages * page
    t_p = s_p + N_META
    t_s = db * s_s
    t_all = t_p + t_s
    topk_p = min(TOPK_MAX, t_p // 4)
    topk_s = min(TOPK_MAX, (past_len + s_s) // 4)
    alpha = (2 * depth) ** 0.25

    d_attn = N_HEADS * HEAD_DIM
    d_kv = KV_HEADS * HEAD_DIM
    d_idx = IDX_HEADS * IDX_DIM
    d_conv = w_conv.shape[2]
    segs, offs, _ = _proj_layout(d_attn, d_kv, d_idx, d_conv, d)
    widths = {name: width for name, width, _ in segs}
    ref_order = ["q", "k", "v", "iq", "iw", "ik", "cu", "cb", "cc", "ga", "gb"]
    ref_starts = np.concatenate([[0], np.cumsum([widths[nm] for nm in ref_order])])

    tq = -(-t_p // BQ) * BQ
    tk = -(-tq // LB) * LB
    tile = max(TM, TM_MOE)
    tp = -(-max(t_all, tk) // tile) * tile
    nqb, nkb = tq // BQ, tk // LB

    pos = jnp.concatenate([jnp.arange(t_p, dtype=jnp.int32),
                           jnp.tile(past_len + jnp.arange(s_s, dtype=jnp.int32), db),
                           jnp.zeros((tp - t_all,), jnp.int32)])
    cos_t, sin_t = _rope_tables(pos)

    h_p = jnp.concatenate([meta_tokens.astype(x_prompt.dtype), x_prompt[0]], axis=0)
    h = _pad_rows(jnp.concatenate([h_p, x_sample.reshape(t_s, d)], axis=0), tp)

    q_scale = HEAD_DIM ** -0.5 * math.log2(math.e)

    outs = {k: [] for k in ("kp", "vp", "ikp", "cp", "ks", "vs", "iks", "cs")}
    for l in range(depth):
        pieces_w, pieces_b = [], []
        for nm, width, padded in segs:
            a = int(ref_starts[ref_order.index(nm)])
            pieces_w.append(jnp.pad(w_in[l][:, a:a + width], [(0, 0), (0, padded - width)]))
            pieces_b.append(jnp.pad(b_in[l][a:a + width], [(0, padded - width)]))
        w_p = jnp.concatenate(pieces_w, axis=1).astype(BF16)
        b_p = jnp.concatenate(pieces_b)[None, :]

        q_r, k_r, v_r, iq_r, ik_r, iw_r, z, cb, ga, gb = _proj(h, w_p, b_p, cos_t, sin_t, offs, widths)

        qg = (q_r[:tq] * q_scale).astype(BF16).reshape(nqb, BQ, KV_HEADS, Q_PER_KV, HEAD_DIM)
        qg = qg.transpose(0, 2, 3, 1, 4).reshape(nqb, KV_HEADS, Q_PER_KV * BQ, HEAD_DIM)
        iqs = iq_r[:tq].astype(BF16).reshape(nqb, BQ, IDX_HEADS, IDX_DIM)
        iqs = iqs.transpose(0, 2, 1, 3).reshape(nqb, IDX_HEADS * BQ, IDX_DIM)
        ikt = ik_r[:tk].astype(BF16).reshape(nkb, LB, IDX_DIM).transpose(0, 2, 1)
        kt = k_r[:tk].astype(BF16).reshape(nkb, LB, d_kv).transpose(0, 2, 1)
        vv = v_r[:tk].astype(BF16).reshape(nkb, LB, d_kv)
        attn_p = _pattn(qg, iqs, iw_r[:tq], ikt, kt, vv, topk_p)

        sl = slice(t_p, t_all)
        iq_s = iq_r[sl].astype(BF16).reshape(db, s_s * IDX_HEADS, IDX_DIM)
        iw_rep = jnp.broadcast_to(iw_r[sl].reshape(db, s_s * IDX_HEADS, 1), (db, s_s * IDX_HEADS, LANES))
        q_s = (q_r[sl] * q_scale).astype(BF16).reshape(db, s_s, KV_HEADS, Q_PER_KV, HEAD_DIM)
        q_s = q_s.transpose(0, 2, 3, 1, 4)
        eye = jnp.eye(KV_HEADS, dtype=BF16)
        qbd = (q_s[:, :, :, :, None, :] * eye[None, :, None, None, :, None])
        qbd = qbd.reshape(db, KV_HEADS * Q_PER_KV * s_s, d_kv)
        new_page_t = lambda a: jnp.pad(a[sl].reshape(db, s_s, a.shape[1]).transpose(0, 2, 1),
                                       [(0, 0), (0, 0), (0, page - s_s)])
        cache_ikt = cache_idx_k[l].transpose(0, 2, 1)
        cache_kt = cache_k[l].transpose(0, 2, 3, 1).reshape(n_phys, d_kv, page)
        cache_vt = cache_v[l].transpose(0, 2, 3, 1).reshape(n_phys, d_kv, page)
        r_s = _sattn(page_table, cache_ikt, cache_kt, cache_vt, iq_s, iw_rep, qbd,
                     new_page_t(ik_r), new_page_t(k_r), new_page_t(v_r), topk_s)
        r_s = r_s.reshape(db, KV_HEADS, Q_PER_KV, s_s, KV_HEADS, HEAD_DIM)
        attn_s = jnp.stack([r_s[:, kvh, :, :, kvh, :] for kvh in range(KV_HEADS)], axis=1)
        attn_s = attn_s.transpose(0, 3, 1, 2, 4).reshape(t_s, d_attn)

        attn_o = _pad_rows(jnp.concatenate([attn_p[:t_p], attn_s], axis=0), tp)

        zp = jnp.concatenate([jnp.zeros((CONV_WIDTH - 1, d_conv), F32), z[:t_p]], axis=0)
        zs = jnp.concatenate([state_conv[l].astype(F32), z[sl].reshape(db, s_s, d_conv)], axis=1)
        z1 = _pad_rows(jnp.concatenate([zp[1:1 + t_p], zs[:, 1:1 + s_s].reshape(t_s, d_conv)], axis=0), tp)
        z2 = _pad_rows(jnp.concatenate([zp[0:t_p], zs[:, 0:s_s].reshape(t_s, d_conv)], axis=0), tp)

        wr = jnp.pad(jnp.concatenate([w_group[l], w_expert_router[l]], axis=1),
                     [(0, 0), (0, LANES - N_GROUPS - N_EXPERTS)])
        br = jnp.pad(jnp.concatenate([b_group[l], b_expert_router[l]]), [(0, LANES - N_GROUPS - N_EXPERTS)])[None]
        h1, comb = _merge(h, attn_o, z, z1, z2, cb, ga, gb, w_conv[l],
                          w_attn_up[l].astype(BF16), w_conv_out[l].astype(BF16), w_o[l].astype(BF16),
                          ln1_g[l][None], ln1_b[l][None], wr, br, alpha)

        wgu = jnp.concatenate([w_gate[l], w_up[l]], axis=2).astype(BF16)
        h = _moe(h1, comb, wgu, w_down[l].astype(BF16), ln2_g[l][None], ln2_b[l][None], alpha)

        outs["kp"].append(k_r[:t_p].reshape(bsz, t_p, KV_HEADS, HEAD_DIM))
        outs["vp"].append(v_r[:t_p].reshape(bsz, t_p, KV_HEADS, HEAD_DIM))
        outs["ikp"].append(ik_r[:t_p].reshape(bsz, t_p, IDX_DIM))
        outs["cp"].append(zp[-(CONV_WIDTH - 1):].reshape(bsz, CONV_WIDTH - 1, d_conv))
        outs["ks"].append(k_r[sl].reshape(db, s_s, KV_HEADS, HEAD_DIM))
        outs["vs"].append(v_r[sl].reshape(db, s_s, KV_HEADS, HEAD_DIM))
        outs["iks"].append(ik_r[sl].reshape(db, s_s, IDX_DIM))
        outs["cs"].append(zs[:, -(CONV_WIDTH - 1):])

    y_prompt = h[N_META:t_p].reshape(bsz, s_p, d)
    y_sample = h[t_p:t_all].reshape(db, s_s, d)
    st = lambda k: jnp.stack(outs[k])
    return (y_prompt, y_sample, st("kp"), st("vp"), st("ikp"), st("cp"), st("ks"), st("vs"), st("iks"), st("cs"))
```

```python
import functools
import math

import numpy as np
import jax
import jax.numpy as jnp
from jax import lax
from jax.experimental import pallas as pl
from jax.experimental.pallas import tpu as pltpu

F32 = jnp.float32
BF16 = jnp.bfloat16
I32 = jnp.int32

N_META = 16
N_HEADS = 8
HEAD_DIM = 64
KV_HEADS = 4
Q_PER_KV = N_HEADS // KV_HEADS
IDX_HEADS = 8
IDX_DIM = 64
TOPK_MAX = 256
CONV_WIDTH = 3
N_GROUPS = 4
EXPERTS_PER_GROUP = 4
N_EXPERTS = N_GROUPS * EXPERTS_PER_GROUP
ROPE_THETA = 10000.0
LN_EPS = 1e-5
IDX_W_SCALE = (IDX_HEADS ** -0.5) * (IDX_DIM ** -0.5)

LANES = 128
SUBLANES = 8
INT_MIN = -(2 ** 31)
KEY_NEG_FLT_MAX = INT_MIN + (1 << 23)
FLT_MAX = float(np.finfo(np.float32).max)
NEG = -1e30
VMEM_LIMIT = 56 * 1024 * 1024

TM = 256
TM_MOE = 512
BQ = 256
LB = 512
ROW_GROUP = 128
PAGES_PER_STEP = 8


def _tile_lanes(x, n):
    reps = n // x.shape[1]
    return x if reps == 1 else jnp.concatenate([x] * reps, axis=1)


def _layer_norm(x, g, b):
    mu = jnp.mean(x, axis=-1, keepdims=True)
    xc = x - mu
    var = jnp.mean(xc * xc, axis=-1, keepdims=True)
    return xc * lax.rsqrt(var + LN_EPS) * g + b


def _proj_layout(d_attn, d_kv, d_idx, d_conv, d_model):
    segs = [("q", d_attn, d_attn), ("k", d_kv, d_kv), ("v", d_kv, d_kv), ("iq", d_idx, d_idx),
            ("iw", IDX_HEADS, LANES), ("ik", IDX_DIM, LANES),
            ("cu", d_conv, d_conv), ("cb", d_conv, d_conv), ("cc", d_conv, d_conv),
            ("ga", d_model, d_model), ("gb", d_model, d_model)]
    offs, o = {}, 0
    for name, _, padded in segs:
        offs[name] = (o, padded)
        o += padded
    return segs, offs, o


def _proj_kernel(x_ref, w_ref, b_ref, cos_ref, sin_ref,
                 q_ref, iq_ref, k_ref, v_ref, ik_ref, kt_ref, ikt_ref, vb_ref, iw_ref, z_ref, cb_ref, ga_ref, gb_ref,
                 *, offs, q_scale):
    xb = x_ref[...].astype(BF16)

    def seg(name):
        a, n = offs[name]
        return jnp.dot(xb, w_ref[:, a:a + n], preferred_element_type=F32) + b_ref[:, a:a + n]

    cos = cos_ref[...]
    sin = sin_ref[...]

    def rope(y):
        n = y.shape[1]
        lane = lax.broadcasted_iota(I32, y.shape, 1)
        first_half = (lane & (HEAD_DIM - 1)) < (HEAD_DIM // 2)
        swapped = jnp.where(first_half, pltpu.roll(y, n - HEAD_DIM // 2, 1), pltpu.roll(y, HEAD_DIM // 2, 1))
        return y * _tile_lanes(cos, n) + swapped * _tile_lanes(sin, n)

    q_ref[...] = rope(seg("q")) * q_scale
    iq_ref[...] = rope(seg("iq"))
    k = rope(seg("k"))
    k_ref[...] = k
    kt_ref[...] = k.T.astype(BF16)
    v = seg("v")
    v_ref[...] = v
    vb_ref[...] = v.astype(BF16)
    ik = rope(seg("ik"))
    ik_ref[...] = ik[:, :IDX_DIM]
    ikt_ref[...] = ik.T[:IDX_DIM].astype(BF16)
    iw_ref[...] = seg("iw")[:, :IDX_HEADS] * IDX_W_SCALE
    z_ref[...] = seg("cc") * seg("cu")
    cb_ref[...] = seg("cb")
    ga_ref[...] = seg("ga")
    gb_ref[...] = seg("gb")


def _proj(x, w, b, cos, sin, offs, widths, q_scale):
    tp, d = x.shape
    n = w.shape[1]
    tm = LB
    nb = tp // tm
    row = lambda width: pl.BlockSpec((tm, width), lambda i: (i, 0))
    blk = lambda depth: pl.BlockSpec((None, depth, tm), lambda i: (i, 0, 0))
    const = lambda shape: pl.BlockSpec(shape, lambda i: (0, 0), pipeline_mode=pl.Buffered(1))
    dq, dkv, dc, dm = widths["q"], widths["k"], widths["cu"], widths["ga"]
    f32 = lambda width: jax.ShapeDtypeStruct((tp, width), F32)
    out = [(row(dq), f32(dq)), (row(widths["iq"]), f32(widths["iq"])),
           (row(dkv), f32(dkv)), (row(dkv), f32(dkv)), (row(IDX_DIM), f32(IDX_DIM)),
           (blk(dkv), jax.ShapeDtypeStruct((nb, dkv, tm), BF16)),
           (blk(IDX_DIM), jax.ShapeDtypeStruct((nb, IDX_DIM, tm), BF16)),
           (row(dkv), jax.ShapeDtypeStruct((tp, dkv), BF16)),
           (row(IDX_HEADS), f32(IDX_HEADS)), (row(dc), f32(dc)), (row(dc), f32(dc)),
           (row(dm), f32(dm)), (row(dm), f32(dm))]
    return pl.pallas_call(
        functools.partial(_proj_kernel, offs=offs, q_scale=q_scale),
        grid=(nb,),
        in_specs=[row(d), const((d, n)), const((1, n)), row(LANES), row(LANES)],
        out_specs=[o[0] for o in out],
        out_shape=[o[1] for o in out],
        compiler_params=pltpu.CompilerParams(dimension_semantics=("arbitrary",), vmem_limit_bytes=VMEM_LIMIT),
        name="proj",
    )(x, w, b, cos, sin)


def _key_to_float(c):
    bits = jnp.where(c >= 0, c, c ^ jnp.int32(0x7FFFFFFF))
    return lax.bitcast_convert_type(bits, F32)


def _float_to_key(x):
    bits = lax.bitcast_convert_type(x, I32)
    return jnp.where(bits >= 0, bits, bits ^ jnp.int32(0x7FFFFFFF))


def _select_topk(s_ref, thr_ref, ntiles, topk, idx_bits, rg):
    _, rows, width = s_ref.shape
    nslab = width // LANES
    nbin = -(-topk // LANES)
    assert nbin <= nslab and rows % rg == 0
    static = isinstance(ntiles, int)

    def loop(body, init):
        if not static:
            return lax.fori_loop(0, ntiles, body, init)
        carry = init
        for t in range(ntiles):
            carry = body(t, carry)
        return carry

    for r0 in range(0, rows, rg):
        lane = lax.broadcasted_iota(I32, (rg, LANES), 1)
        slab = lambda t, j: s_ref[t, r0:r0 + rg, j * LANES:(j + 1) * LANES]

        def count(pred):
            def body(t, cnt):
                for j in range(nslab):
                    cnt = cnt + jnp.where(pred(slab(t, j), t * width + j * LANES + lane), 1.0, 0.0)
                return cnt
            return jnp.sum(loop(body, jnp.zeros((rg, LANES), F32)), axis=1, keepdims=True)

        def bin_body(t, bm):
            bm = list(bm)
            for j in range(nslab):
                bm[j % nbin] = jnp.maximum(bm[j % nbin], slab(t, j))
            return tuple(bm)

        bm = loop(bin_body, tuple(jnp.full((rg, LANES), -jnp.inf, F32) for _ in range(nbin)))
        lo, hi = bm[0], bm[0]
        for j in range(1, nbin):
            lo, hi = jnp.minimum(lo, bm[j]), jnp.maximum(hi, bm[j])
        k_lo = _float_to_key(jnp.min(lo, axis=1, keepdims=True))
        k_hi = _float_to_key(jnp.max(hi, axis=1, keepdims=True))
        nb_rows = (32 - lax.clz(k_lo ^ k_hi)).astype(F32)
        nbits = jnp.max(nb_rows).astype(I32)
        himask = jnp.where(nbits >= 32, jnp.int32(0), lax.shift_left(jnp.int32(-1), jnp.minimum(nbits, 31)))
        t0 = jnp.broadcast_to(((k_hi ^ jnp.int32(INT_MIN)) & himask) ^ jnp.int32(INT_MIN), (rg, LANES))

        def cond(st):
            b, _, done = st
            return jnp.logical_and(b >= 0, jnp.min(done) < 0.5)

        def body(st):
            b, t, done = st
            cand = t + lax.shift_left(jnp.int32(1), b)
            cf = _key_to_float(cand)
            cnt = count(lambda s, idx: s >= cf)
            t = jnp.where(cnt >= topk, jnp.where(done < 0.5, cand, t), t)
            done = jnp.where(cnt == topk, 1.0, done)
            return b - 1, t, done

        _, t, done = lax.while_loop(cond, body, (nbits - 1, t0, jnp.zeros((rg, LANES), F32)))
        thr = jnp.where(t < KEY_NEG_FLT_MAX, -FLT_MAX, _key_to_float(t))
        thr_ref[r0:r0 + rg, :] = thr

        @pl.when(jnp.min(done) < 0.5)
        def _():
            c_gt = count(lambda s, idx: s > thr)
            c_ge = count(lambda s, idx: s >= thr)
            need = topk - c_gt

            @pl.when(jnp.max(c_ge) > topk)
            def _():
                def index_bit(bi, x):
                    cand = x + lax.shift_left(jnp.int32(1), idx_bits - 1 - bi)
                    cnt = count(lambda s, idx: jnp.where(s == thr, idx, cand) < cand)
                    return jnp.where(cnt < need, cand, x)

                x = lax.fori_loop(0, idx_bits, index_bit, jnp.zeros((rg, LANES), I32))

                def fix(t_, c):
                    for j in range(nslab):
                        sj = slab(t_, j)
                        drop = jnp.where(sj == thr, t_ * width + j * LANES + lane, x) > x
                        s_ref[t_, r0:r0 + rg, j * LANES:(j + 1) * LANES] = jnp.where(drop, -jnp.inf, sj)
                    return c

                loop(fix, 0)


def _pattn_kernel(q_ref, iq_ref, iw_ref, ikt_ref, kt_ref, v_ref, o_ref,
                  s_ref, thr_ref, wb_ref, qg_ref, iqh_ref, m_ref, l_ref, acc_ref, *, topk, idx_bits):
    bq = o_ref.shape[0]
    lb = s_ref.shape[2]
    i = pl.program_id(0)
    nkb = pl.cdiv((i + 1) * bq, lb)

    for h in range(IDX_HEADS):
        wb_ref[h] = jnp.broadcast_to(iw_ref[:, h:h + 1], (bq, LANES))
        iqh_ref[h * bq:(h + 1) * bq, :] = iq_ref[:, h * IDX_DIM:(h + 1) * IDX_DIM].astype(BF16)
    for h in range(N_HEADS):
        kvh, g = divmod(h, Q_PER_KV)
        qg_ref[kvh, g * bq:(g + 1) * bq, :] = q_ref[:, h * HEAD_DIM:(h + 1) * HEAD_DIM].astype(BF16)

    row = lax.broadcasted_iota(I32, (bq, lb), 0)
    col = lax.broadcasted_iota(I32, (bq, lb), 1)

    def scores(kb, c):
        d = jnp.dot(iqh_ref[...], ikt_ref[kb], preferred_element_type=F32)
        acc = jnp.zeros((bq, lb), F32)
        for h in range(IDX_HEADS):
            acc = acc + _tile_lanes(wb_ref[h], lb) * jnp.maximum(d[h * bq:(h + 1) * bq], 0.0)
        s_ref[kb] = jnp.where(kb * lb + col <= i * bq + row, acc, -jnp.inf)
        return c

    lax.fori_loop(0, nkb, scores, 0)

    _select_topk(s_ref, thr_ref, nkb, topk, idx_bits, ROW_GROUP)
    thr_full = _tile_lanes(thr_ref[...], lb)

    m_ref[...] = jnp.full(m_ref.shape, NEG, F32)
    l_ref[...] = jnp.zeros(l_ref.shape, F32)
    acc_ref[...] = jnp.zeros(acc_ref.shape, F32)

    def attend(kb, c):
        bias = jnp.where(s_ref[kb] >= thr_full, 0.0, NEG)
        bias2 = jnp.concatenate([bias] * Q_PER_KV, axis=0)
        for kvh in range(KV_HEADS):
            s = jnp.dot(qg_ref[kvh], kt_ref[kb, kvh * HEAD_DIM:(kvh + 1) * HEAD_DIM, :],
                        preferred_element_type=F32) + bias2
            m_prev = m_ref[kvh]
            m_new = jnp.maximum(m_prev, jnp.max(s, axis=1, keepdims=True))
            alpha = jnp.exp2(m_prev - m_new)
            p = jnp.exp2(s - _tile_lanes(m_new, lb))
            psum = p[:, :LANES]
            for j in range(1, lb // LANES):
                psum = psum + p[:, j * LANES:(j + 1) * LANES]
            l_ref[kvh] = alpha * l_ref[kvh] + psum
            pair = kvh // 2
            pv = jnp.dot(p.astype(BF16), v_ref[kb, :, pair * LANES:(pair + 1) * LANES],
                         preferred_element_type=F32)
            acc_ref[kvh] = alpha * acc_ref[kvh] + pv
            m_ref[kvh] = m_new
        return c

    lax.fori_loop(0, nkb, attend, 0)

    for kvh in range(KV_HEADS):
        denom = jnp.sum(l_ref[kvh], axis=1, keepdims=True)
        off = (kvh % 2) * HEAD_DIM
        o = acc_ref[kvh][:, off:off + HEAD_DIM] / denom
        for g in range(Q_PER_KV):
            h = kvh * Q_PER_KV + g
            o_ref[:, h * HEAD_DIM:(h + 1) * HEAD_DIM] = o[g * bq:(g + 1) * bq]


def _pattn(q, iq, iw, ikt, kt, vv, topk, nqb):
    bq = BQ
    rows2 = Q_PER_KV * bq
    lb = ikt.shape[2]
    tq = nqb * bq
    nkb = pl.cdiv(tq, lb)
    assert nkb <= ikt.shape[0]
    idx_bits = int(nkb * lb).bit_length()
    resident = lambda shape: pl.BlockSpec(shape, lambda i: (0,) * len(shape), pipeline_mode=pl.Buffered(1))
    return pl.pallas_call(
        functools.partial(_pattn_kernel, topk=topk, idx_bits=idx_bits),
        grid=(nqb,),
        in_specs=[
            pl.BlockSpec((bq, q.shape[1]), lambda i: (i, 0)),
            pl.BlockSpec((bq, iq.shape[1]), lambda i: (i, 0)),
            pl.BlockSpec((bq, IDX_HEADS), lambda i: (i, 0)),
            resident(ikt.shape), resident(kt.shape), resident(vv.shape),
        ],
        out_specs=pl.BlockSpec((bq, N_HEADS * HEAD_DIM), lambda i: (i, 0)),
        out_shape=jax.ShapeDtypeStruct((tq, N_HEADS * HEAD_DIM), F32),
        scratch_shapes=[
            pltpu.VMEM((nkb, bq, lb), F32),
            pltpu.VMEM((bq, LANES), F32),
            pltpu.VMEM((IDX_HEADS, bq, LANES), F32),
            pltpu.VMEM((KV_HEADS, rows2, HEAD_DIM), BF16),
            pltpu.VMEM((IDX_HEADS * bq, IDX_DIM), BF16),
            pltpu.VMEM((KV_HEADS, rows2, LANES), F32),
            pltpu.VMEM((KV_HEADS, rows2, LANES), F32),
            pltpu.VMEM((KV_HEADS, rows2, LANES), F32),
        ],
        compiler_params=pltpu.CompilerParams(dimension_semantics=("arbitrary",), vmem_limit_bytes=VMEM_LIMIT),
        name="pattn",
    )(q, iq, iw, ikt, kt, vv)


def _sattn_kernel(pt_ref, *refs, topk, idx_bits, s_new):
    del pt_ref
    np_ = PAGES_PER_STEP
    cik = refs[0:np_]
    ck = refs[np_:2 * np_]
    cv = refs[2 * np_:3 * np_]
    iq_ref, iw_ref, qbd_ref, nik_ref, nk_ref, nv_ref, o_ref, s_ref, thr_ref, lg_ref, vst_ref = refs[3 * np_:]
    b = pl.program_id(0)
    jj = pl.program_id(1)
    nsteps = pl.num_programs(1)
    ntiles, rows, width = s_ref.shape
    page = width // np_
    reps = rows // s_new

    @pl.when(jnp.logical_and(b == 0, jj == 0))
    def _():
        s_ref[ntiles - 1] = jnp.full((rows, width), -jnp.inf, F32)
        lg_ref[ntiles - 1] = jnp.zeros(lg_ref.shape[1:], F32)
        vst_ref[ntiles - 1] = jnp.zeros(vst_ref.shape[1:], BF16)

    def do_page(t, p, ikt, kt, vt, is_new):
        d = jnp.dot(iq_ref[...], ikt.astype(BF16), preferred_element_type=F32)
        r = jnp.maximum(d, 0.0) * iw_ref[...]
        per_q = [jnp.sum(r[q * IDX_HEADS:(q + 1) * IDX_HEADS], axis=0, keepdims=True) for q in range(s_new)]
        sc = jnp.concatenate(per_q * reps, axis=0)
        if is_new:
            qi = lax.broadcasted_iota(I32, sc.shape, 0) & (s_new - 1)
            ki = lax.broadcasted_iota(I32, sc.shape, 1)
            sc = jnp.where(ki <= qi, sc, -jnp.inf)
        s_ref[t, :, p * page:(p + 1) * page] = sc
        lg_ref[t, :, p * page:(p + 1) * page] = jnp.dot(qbd_ref[...], kt.astype(BF16),
                                                        preferred_element_type=F32)
        vst_ref[t, :, p * page:(p + 1) * page] = vt.astype(BF16)

    for p in range(np_):
        do_page(jj, p, cik[p][...], ck[p][...], cv[p][...], False)

    @pl.when(jj == nsteps - 1)
    def _():
        do_page(ntiles - 1, 0, nik_ref[...], nk_ref[...], nv_ref[...], True)
        _select_topk(s_ref, thr_ref, ntiles, topk, idx_bits, rows)
        thr_full = _tile_lanes(thr_ref[...], width)
        nvreg = lg_ref.shape[1] // rows

        def masked(t):
            bias = jnp.where(s_ref[t] >= thr_full, 0.0, NEG)
            return lg_ref[t] + jnp.concatenate([bias] * nvreg, axis=0)

        m = jnp.max(masked(0), axis=1, keepdims=True)
        for t in range(1, ntiles):
            m = jnp.maximum(m, jnp.max(masked(t), axis=1, keepdims=True))
        l = jnp.zeros((lg_ref.shape[1], 1), F32)
        acc = jnp.zeros(o_ref.shape, F32)
        for t in range(ntiles):
            p = jnp.exp2(masked(t) - m)
            l = l + jnp.sum(p, axis=1, keepdims=True)
            acc = acc + lax.dot_general(p.astype(BF16), vst_ref[t], (((1,), (1,)), ((), ())),
                                        preferred_element_type=F32)
        o_ref[...] = acc / l


def _sattn(page_table, cache_ikt, cache_kt, cache_vt, iq_s, iw_rep, qbd, nikt, nkt, nvt, topk):
    db, pages = page_table.shape
    page = cache_ikt.shape[2]
    assert page == LANES and pages % PAGES_PER_STEP == 0
    s_new = iq_s.shape[1] // IDX_HEADS
    rows = SUBLANES
    assert rows % s_new == 0 and s_new & (s_new - 1) == 0
    nrow = qbd.shape[1]
    dkv = cache_kt.shape[1]
    nsteps = pages // PAGES_PER_STEP
    width = PAGES_PER_STEP * page
    idx_bits = int((nsteps + 1) * width).bit_length()

    def page_spec(depth, p):
        return pl.BlockSpec((None, depth, page),
                            lambda b, jj, pt, p=p: (pt[b * pages + jj * PAGES_PER_STEP + p], 0, 0))

    per_seq = lambda shape: pl.BlockSpec((None,) + shape, lambda b, jj, pt: (b, 0, 0))
    in_specs = ([page_spec(IDX_DIM, p) for p in range(PAGES_PER_STEP)]
                + [page_spec(dkv, p) for p in range(PAGES_PER_STEP)]
                + [page_spec(dkv, p) for p in range(PAGES_PER_STEP)]
                + [per_seq(iq_s.shape[1:]), per_seq(iw_rep.shape[1:]), per_seq(qbd.shape[1:]),
                   per_seq(nikt.shape[1:]), per_seq(nkt.shape[1:]), per_seq(nvt.shape[1:])])
    grid_spec = pltpu.PrefetchScalarGridSpec(
        num_scalar_prefetch=1,
        grid=(db, nsteps),
        in_specs=in_specs,
        out_specs=pl.BlockSpec((None, nrow, dkv), lambda b, jj, pt: (b, 0, 0)),
        scratch_shapes=[
            pltpu.VMEM((nsteps + 1, rows, width), F32),
            pltpu.VMEM((rows, LANES), F32),
            pltpu.VMEM((nsteps + 1, nrow, width), F32),
            pltpu.VMEM((nsteps + 1, dkv, width), BF16),
        ],
    )
    args = ([cache_ikt] * PAGES_PER_STEP + [cache_kt] * PAGES_PER_STEP + [cache_vt] * PAGES_PER_STEP
            + [iq_s, iw_rep, qbd, nikt, nkt, nvt])
    return pl.pallas_call(
        functools.partial(_sattn_kernel, topk=topk, idx_bits=idx_bits, s_new=s_new),
        grid_spec=grid_spec,
        out_shape=jax.ShapeDtypeStruct((db, nrow, dkv), F32),
        compiler_params=pltpu.CompilerParams(dimension_semantics=("arbitrary", "arbitrary"),
                                             vmem_limit_bytes=VMEM_LIMIT),
        name="sattn",
    )(page_table.reshape(-1), *args)


def _merge_kernel(x_ref, attn_ref, z_ref, z1_ref, z2_ref, cb_ref, ga_ref, gb_ref,
                  wc_ref, wa_ref, wb_ref, wo_ref, g_ref, b_ref, wr_ref, br_ref,
                  h_ref, comb_ref, *, alpha):
    wc = wc_ref[...]
    y = wc[0:1] * z2_ref[...] + wc[1:2] * z1_ref[...] + wc[2:3] * z_ref[...]
    conv_o = cb_ref[...] * y
    a = jnp.dot(attn_ref[...].astype(BF16), wa_ref[...], preferred_element_type=F32)
    b = jnp.dot(conv_o.astype(BF16), wb_ref[...], preferred_element_type=F32)
    sig = lambda u: 1.0 / (1.0 + jnp.exp(-u))
    mixed = sig(ga_ref[...]) * a + sig(gb_ref[...]) * b
    mix = jnp.dot(mixed.astype(BF16), wo_ref[...], preferred_element_type=F32)
    h = _layer_norm(alpha * x_ref[...] + mix, g_ref[...], b_ref[...])
    h_ref[...] = h

    logits = jnp.dot(h, wr_ref[...], preferred_element_type=F32, precision=lax.Precision.HIGHEST) + br_ref[...]
    lane_i = lax.broadcasted_iota(I32, logits.shape, 1)
    lane = lane_i.astype(F32)
    first_where = lambda cond: jnp.min(jnp.where(cond, lane, float(LANES)), axis=1, keepdims=True)
    is_g = lane_i < N_GROUPS
    gl = jnp.where(is_g, logits, -jnp.inf)
    gmax = jnp.max(gl, axis=1, keepdims=True)
    g_sel = first_where(gl == gmax)
    g_p = 1.0 / jnp.sum(jnp.where(is_g, jnp.exp(logits - gmax), 0.0), axis=1, keepdims=True)
    grp = jnp.where((lane_i >= N_GROUPS) & (lane_i < N_GROUPS + N_EXPERTS),
                    lax.shift_right_arithmetic(lane_i - N_GROUPS, jnp.int32(int(math.log2(EXPERTS_PER_GROUP)))),
                    -1).astype(F32)
    in_grp = grp == g_sel
    e1 = jnp.where(in_grp, logits, -jnp.inf)
    max1 = jnp.max(e1, axis=1, keepdims=True)
    i1 = first_where(e1 == max1)
    e2 = jnp.where(lane == i1, -jnp.inf, e1)
    max2 = jnp.max(e2, axis=1, keepdims=True)
    i2 = first_where(e2 == max2)
    den = jnp.sum(jnp.where(in_grp, jnp.exp(logits - max1), 0.0), axis=1, keepdims=True)
    p1 = 1.0 / den
    p2 = jnp.exp(max2 - max1) / den
    tot = p1 + p2
    comb_ref[...] = jnp.where(lane == i1, p1 / tot * g_p, 0.0) + jnp.where(lane == i2, p2 / tot * g_p, 0.0)


def _merge(x, attn_o, z, z1, z2, cb, ga, gb, wc, wa, wb, wo, g, b, wr, br, alpha):
    tp, d = x.shape
    row = lambda width: pl.BlockSpec((TM, width), lambda i: (i, 0))
    const = lambda a: pl.BlockSpec(a.shape, lambda i: (0, 0), pipeline_mode=pl.Buffered(1))
    dc = z.shape[1]
    return pl.pallas_call(
        functools.partial(_merge_kernel, alpha=alpha),
        grid=(tp // TM,),
        in_specs=[row(d), row(attn_o.shape[1]), row(dc), row(dc), row(dc), row(dc), row(d), row(d),
                  const(wc), const(wa), const(wb), const(wo), const(g), const(b), const(wr), const(br)],
        out_specs=[row(d), row(LANES)],
        out_shape=[jax.ShapeDtypeStruct((tp, d), F32), jax.ShapeDtypeStruct((tp, LANES), F32)],
        compiler_params=pltpu.CompilerParams(dimension_semantics=("arbitrary",), vmem_limit_bytes=VMEM_LIMIT),
        name="merge",
    )(x, attn_o, z, z1, z2, cb, ga, gb, wc, wa, wb, wo, g, b, wr, br)


def _moe_kernel(h_ref, comb_ref, wgu_ref, wd_ref, g_ref, b_ref, o_ref, acc_ref, hb_ref, *, alpha):
    e = pl.program_id(1)
    de = wd_ref.shape[0]

    @pl.when(e == 0)
    def _():
        acc_ref[...] = jnp.zeros(acc_ref.shape, F32)
        hb_ref[...] = h_ref[...].astype(BF16)

    gu = jnp.dot(hb_ref[...], wgu_ref[...], preferred_element_type=F32)
    hg = gu[:, :de]
    hu = gu[:, de:]
    lane = lax.broadcasted_iota(I32, comb_ref.shape, 1)
    c = jnp.sum(jnp.where(lane == e + N_GROUPS, comb_ref[...], 0.0), axis=1, keepdims=True)
    act = hg * (1.0 / (1.0 + jnp.exp(-hg))) * hu * c
    acc_ref[...] += jnp.dot(act.astype(BF16), wd_ref[...], preferred_element_type=F32)

    @pl.when(e == pl.num_programs(1) - 1)
    def _():
        o_ref[...] = _layer_norm(alpha * h_ref[...] + acc_ref[...], g_ref[...], b_ref[...])


def _moe(h, comb, wgu, wd, g, b, alpha):
    tp, d = h.shape
    ne, _, de2 = wgu.shape
    de = wd.shape[1]
    return pl.pallas_call(
        functools.partial(_moe_kernel, alpha=alpha),
        grid=(tp // TM_MOE, ne),
        in_specs=[
            pl.BlockSpec((TM_MOE, d), lambda i, e: (i, 0)),
            pl.BlockSpec((TM_MOE, LANES), lambda i, e: (i, 0)),
            pl.BlockSpec((None, d, de2), lambda i, e: (e, 0, 0)),
            pl.BlockSpec((None, de, d), lambda i, e: (e, 0, 0)),
            pl.BlockSpec((1, d), lambda i, e: (0, 0)),
            pl.BlockSpec((1, d), lambda i, e: (0, 0)),
        ],
        out_specs=pl.BlockSpec((TM_MOE, d), lambda i, e: (i, 0)),
        out_shape=jax.ShapeDtypeStruct((tp, d), F32),
        scratch_shapes=[pltpu.VMEM((TM_MOE, d), F32), pltpu.VMEM((TM_MOE, d), BF16)],
        compiler_params=pltpu.CompilerParams(dimension_semantics=("arbitrary", "arbitrary"),
                                             vmem_limit_bytes=VMEM_LIMIT),
        name="moe",
    )(h, comb, wgu, wd, g, b)


def _rope_tables(pos):
    half = HEAD_DIM // 2
    inv = jnp.power(jnp.float32(ROPE_THETA), -jnp.arange(half, dtype=F32) * 2.0 / HEAD_DIM)
    ang = pos.astype(F32)[:, None] * inv[None, :]
    cos, sin = jnp.cos(ang), jnp.sin(ang)
    reps = LANES // HEAD_DIM
    return (jnp.concatenate([cos, cos] * reps, axis=1), jnp.concatenate([-sin, sin] * reps, axis=1))


def _pad_rows(a, rows):
    return jnp.pad(a, [(0, rows - a.shape[0])] + [(0, 0)] * (a.ndim - 1))


def kernel(x_prompt, x_sample, cache_k, cache_v, cache_idx_k, state_conv, page_table, meta_tokens, w_in, b_in, w_conv, w_attn_up, w_conv_out, w_o, ln1_g, ln1_b, w_group, b_group, w_expert_router, b_expert_router, w_gate, w_up, w_down, ln2_g, ln2_b):
    bsz, s_p, d = x_prompt.shape
    db, s_s, _ = x_sample.shape
    depth = w_in.shape[0]
    assert bsz == 1, "prompt group is served one sequence at a time"
    assert s_s >= CONV_WIDTH - 1
    n_phys, page = cache_k.shape[1], cache_k.shape[2]
    pages = page_table.shape[1]
    past_len = pages * page
    t_p = s_p + N_META
    t_s = db * s_s
    t_all = t_p + t_s
    topk_p = min(TOPK_MAX, t_p // 4)
    topk_s = min(TOPK_MAX, (past_len + s_s) // 4)
    alpha = (2 * depth) ** 0.25

    d_attn = N_HEADS * HEAD_DIM
    d_kv = KV_HEADS * HEAD_DIM
    d_idx = IDX_HEADS * IDX_DIM
    d_conv = w_conv.shape[2]
    segs, offs, _ = _proj_layout(d_attn, d_kv, d_idx, d_conv, d)
    widths = {name: width for name, width, _ in segs}
    ref_order = ["q", "k", "v", "iq", "iw", "ik", "cu", "cb", "cc", "ga", "gb"]
    ref_starts = np.concatenate([[0], np.cumsum([widths[nm] for nm in ref_order])])

    tq = -(-t_p // BQ) * BQ
    tk = -(-tq // LB) * LB
    tile = max(TM, TM_MOE, LB)
    tp = -(-max(t_all, tk) // tile) * tile
    nqb = tq // BQ

    pos = jnp.concatenate([jnp.arange(t_p, dtype=jnp.int32),
                           jnp.tile(past_len + jnp.arange(s_s, dtype=jnp.int32), db),
                           jnp.zeros((tp - t_all,), jnp.int32)])
    cos_t, sin_t = _rope_tables(pos)

    h_p = jnp.concatenate([meta_tokens.astype(x_prompt.dtype), x_prompt[0]], axis=0)
    h = _pad_rows(jnp.concatenate([h_p, x_sample.reshape(t_s, d)], axis=0), tp)

    q_scale = HEAD_DIM ** -0.5 * math.log2(math.e)

    outs = {k: [] for k in ("kp", "vp", "ikp", "cp", "ks", "vs", "iks", "cs")}
    for l in range(depth):
        pieces_w, pieces_b = [], []
        for nm, width, padded in segs:
            a = int(ref_starts[ref_order.index(nm)])
            pieces_w.append(jnp.pad(w_in[l][:, a:a + width], [(0, 0), (0, padded - width)]))
            pieces_b.append(jnp.pad(b_in[l][a:a + width], [(0, padded - width)]))
        w_p = jnp.concatenate(pieces_w, axis=1).astype(BF16)
        b_p = jnp.concatenate(pieces_b)[None, :]

        (q_r, iq_r, k_r, v_r, ik_r, kt_b, ikt_b, v_b, iw_r, z, cb, ga, gb) = _proj(
            h, w_p, b_p, cos_t, sin_t, offs, widths, q_scale)

        attn_p = _pattn(q_r, iq_r, iw_r, ikt_b, kt_b, v_b.reshape(tp // LB, LB, d_kv), topk_p, nqb)

        sl = slice(t_p, t_all)
        iq_s = iq_r[sl].astype(BF16).reshape(db, s_s * IDX_HEADS, IDX_DIM)
        iw_rep = jnp.broadcast_to(iw_r[sl].reshape(db, s_s * IDX_HEADS, 1), (db, s_s * IDX_HEADS, LANES))
        q_s = q_r[sl].astype(BF16).reshape(db, s_s, KV_HEADS, Q_PER_KV, HEAD_DIM)
        q_s = q_s.transpose(0, 2, 3, 1, 4)
        eye = jnp.eye(KV_HEADS, dtype=BF16)
        qbd = (q_s[:, :, :, :, None, :] * eye[None, :, None, None, :, None])
        qbd = qbd.reshape(db, KV_HEADS * Q_PER_KV * s_s, d_kv)
        new_page_t = lambda a: jnp.pad(a[sl].reshape(db, s_s, a.shape[1]).transpose(0, 2, 1),
                                       [(0, 0), (0, 0), (0, page - s_s)])
        cache_ikt = cache_idx_k[l].transpose(0, 2, 1)
        cache_kt = cache_k[l].transpose(0, 2, 3, 1).reshape(n_phys, d_kv, page)
        cache_vt = cache_v[l].transpose(0, 2, 3, 1).reshape(n_phys, d_kv, page)
        r_s = _sattn(page_table, cache_ikt, cache_kt, cache_vt, iq_s, iw_rep, qbd,
                     new_page_t(ik_r), new_page_t(k_r), new_page_t(v_r), topk_s)
        r_s = r_s.reshape(db, KV_HEADS, Q_PER_KV, s_s, KV_HEADS, HEAD_DIM)
        attn_s = jnp.stack([r_s[:, kvh, :, :, kvh, :] for kvh in range(KV_HEADS)], axis=1)
        attn_s = attn_s.transpose(0, 3, 1, 2, 4).reshape(t_s, d_attn)

        attn_o = _pad_rows(jnp.concatenate([attn_p[:t_p], attn_s], axis=0), tp)

        zp = jnp.concatenate([jnp.zeros((CONV_WIDTH - 1, d_conv), F32), z[:t_p]], axis=0)
        zs = jnp.concatenate([state_conv[l].astype(F32), z[sl].reshape(db, s_s, d_conv)], axis=1)
        z1 = _pad_rows(jnp.concatenate([zp[1:1 + t_p], zs[:, 1:1 + s_s].reshape(t_s, d_conv)], axis=0), tp)
        z2 = _pad_rows(jnp.concatenate([zp[0:t_p], zs[:, 0:s_s].reshape(t_s, d_conv)], axis=0), tp)

        wr = jnp.pad(jnp.concatenate([w_group[l], w_expert_router[l]], axis=1),
                     [(0, 0), (0, LANES - N_GROUPS - N_EXPERTS)])
        br = jnp.pad(jnp.concatenate([b_group[l], b_expert_router[l]]), [(0, LANES - N_GROUPS - N_EXPERTS)])[None]
        h1, comb = _merge(h, attn_o, z, z1, z2, cb, ga, gb, w_conv[l],
                          w_attn_up[l].astype(BF16), w_conv_out[l].astype(BF16), w_o[l].astype(BF16),
                          ln1_g[l][None], ln1_b[l][None], wr, br, alpha)

        wgu = jnp.concatenate([w_gate[l], w_up[l]], axis=2).astype(BF16)
        h = _moe(h1, comb, wgu, w_down[l].astype(BF16), ln2_g[l][None], ln2_b[l][None], alpha)

        outs["kp"].append(k_r[:t_p].reshape(bsz, t_p, KV_HEADS, HEAD_DIM))
        outs["vp"].append(v_r[:t_p].reshape(bsz, t_p, KV_HEADS, HEAD_DIM))
        outs["ikp"].append(ik_r[:t_p].reshape(bsz, t_p, IDX_DIM))
        outs["cp"].append(zp[-(CONV_WIDTH - 1):].reshape(bsz, CONV_WIDTH - 1, d_conv))
        outs["ks"].append(k_r[sl].reshape(db, s_s, KV_HEADS, HEAD_DIM))
        outs["vs"].append(v_r[sl].reshape(db, s_s, KV_HEADS, HEAD_DIM))
        outs["iks"].append(ik_r[sl].reshape(db, s_s, IDX_DIM))
        outs["cs"].append(zs[:, -(CONV_WIDTH - 1):])

    y_prompt = h[N_META:t_p].reshape(bsz, s_p, d)
    y_sample = h[t_p:t_all].reshape(db, s_s, d)
    st = lambda k: jnp.stack(outs[k])
    return (y_prompt, y_sample, st("kp"), st("vp"), st("ikp"), st("cp"), st("ks"), st("vs"), st("iks"), st("cs"))
```

```python
import functools
import math

import numpy as np
import jax
import jax.numpy as jnp
from jax import lax
from jax.experimental import pallas as pl
from jax.experimental.pallas import tpu as pltpu

F32 = jnp.float32
BF16 = jnp.bfloat16
I32 = jnp.int32

N_META = 16
N_HEADS = 8
HEAD_DIM = 64
KV_HEADS = 4
Q_PER_KV = N_HEADS // KV_HEADS
IDX_HEADS = 8
IDX_DIM = 64
TOPK_MAX = 256
CONV_WIDTH = 3
N_GROUPS = 4
EXPERTS_PER_GROUP = 4
N_EXPERTS = N_GROUPS * EXPERTS_PER_GROUP
ROPE_THETA = 10000.0
LN_EPS = 1e-5
IDX_W_SCALE = (IDX_HEADS ** -0.5) * (IDX_DIM ** -0.5)

LANES = 128
SUBLANES = 8
INT_MIN = -(2 ** 31)
KEY_NEG_FLT_MAX = INT_MIN + (1 << 23)
FLT_MAX = float(np.finfo(np.float32).max)
NEG = -1e30
VMEM_LIMIT = 56 * 1024 * 1024

TM = 256
TM_MOE = 1024
BQ = 256
LB = 512
ROW_GROUP = 128
PAGES_PER_STEP = 8
SEQS_PER_GROUP = 4


def _tile_lanes(x, n):
    reps = n // x.shape[1]
    return x if reps == 1 else jnp.concatenate([x] * reps, axis=1)


def _layer_norm(x, g, b):
    mu = jnp.mean(x, axis=-1, keepdims=True)
    xc = x - mu
    var = jnp.mean(xc * xc, axis=-1, keepdims=True)
    return xc * lax.rsqrt(var + LN_EPS) * g + b


def _proj_layout(d_attn, d_kv, d_idx, d_conv, d_model):
    segs = [("q", d_attn, d_attn), ("k", d_kv, d_kv), ("v", d_kv, d_kv), ("iq", d_idx, d_idx),
            ("iw", IDX_HEADS, LANES), ("ik", IDX_DIM, LANES),
            ("cu", d_conv, d_conv), ("cb", d_conv, d_conv), ("cc", d_conv, d_conv),
            ("ga", d_model, d_model), ("gb", d_model, d_model)]
    offs, o = {}, 0
    for name, _, padded in segs:
        offs[name] = (o, padded)
        o += padded
    return segs, offs, o


def _proj_kernel(x_ref, w_ref, b_ref, cos_ref, sin_ref,
                 q_ref, iq_ref, k_ref, v_ref, ik_ref, kt_ref, ikt_ref, vb_ref, iw_ref, z_ref, cb_ref, ga_ref, gb_ref,
                 *, offs, q_scale):
    xb = x_ref[...].astype(BF16)

    def seg(name):
        a, n = offs[name]
        return jnp.dot(xb, w_ref[:, a:a + n], preferred_element_type=F32) + b_ref[:, a:a + n]

    cos = cos_ref[...]
    sin = sin_ref[...]

    def rope(y):
        n = y.shape[1]
        lane = lax.broadcasted_iota(I32, y.shape, 1)
        first_half = (lane & (HEAD_DIM - 1)) < (HEAD_DIM // 2)
        swapped = jnp.where(first_half, pltpu.roll(y, n - HEAD_DIM // 2, 1), pltpu.roll(y, HEAD_DIM // 2, 1))
        return y * _tile_lanes(cos, n) + swapped * _tile_lanes(sin, n)

    q_ref[...] = rope(seg("q")) * q_scale
    iq_ref[...] = rope(seg("iq"))
    k = rope(seg("k"))
    k_ref[...] = k
    kt_ref[...] = k.T.astype(BF16)
    v = seg("v")
    v_ref[...] = v
    vb_ref[...] = v.astype(BF16)
    ik = rope(seg("ik"))
    ik_ref[...] = ik[:, :IDX_DIM]
    ikt_ref[...] = ik.T[:IDX_DIM].astype(BF16)
    iw_ref[...] = seg("iw")[:, :IDX_HEADS] * IDX_W_SCALE
    z_ref[...] = seg("cc") * seg("cu")
    cb_ref[...] = seg("cb")
    ga_ref[...] = seg("ga")
    gb_ref[...] = seg("gb")


def _proj(x, w, b, cos, sin, offs, widths, q_scale):
    tp, d = x.shape
    n = w.shape[1]
    tm = LB
    nb = tp // tm
    row = lambda width: pl.BlockSpec((tm, width), lambda i: (i, 0))
    blk = lambda depth: pl.BlockSpec((None, depth, tm), lambda i: (i, 0, 0))
    const = lambda shape: pl.BlockSpec(shape, lambda i: (0, 0), pipeline_mode=pl.Buffered(1))
    dq, dkv, dc, dm = widths["q"], widths["k"], widths["cu"], widths["ga"]
    f32 = lambda width: jax.ShapeDtypeStruct((tp, width), F32)
    out = [(row(dq), f32(dq)), (row(widths["iq"]), f32(widths["iq"])),
           (row(dkv), f32(dkv)), (row(dkv), f32(dkv)), (row(IDX_DIM), f32(IDX_DIM)),
           (blk(dkv), jax.ShapeDtypeStruct((nb, dkv, tm), BF16)),
           (blk(IDX_DIM), jax.ShapeDtypeStruct((nb, IDX_DIM, tm), BF16)),
           (row(dkv), jax.ShapeDtypeStruct((tp, dkv), BF16)),
           (row(IDX_HEADS), f32(IDX_HEADS)), (row(dc), f32(dc)), (row(dc), f32(dc)),
           (row(dm), f32(dm)), (row(dm), f32(dm))]
    return pl.pallas_call(
        functools.partial(_proj_kernel, offs=offs, q_scale=q_scale),
        grid=(nb,),
        in_specs=[row(d), const((d, n)), const((1, n)), row(LANES), row(LANES)],
        out_specs=[o[0] for o in out],
        out_shape=[o[1] for o in out],
        compiler_params=pltpu.CompilerParams(dimension_semantics=("arbitrary",), vmem_limit_bytes=VMEM_LIMIT),
        name="proj",
    )(x, w, b, cos, sin)


def _key_to_float(c):
    bits = jnp.where(c >= 0, c, c ^ jnp.int32(0x7FFFFFFF))
    return lax.bitcast_convert_type(bits, F32)


def _float_to_key(x):
    bits = lax.bitcast_convert_type(x, I32)
    return jnp.where(bits >= 0, bits, bits ^ jnp.int32(0x7FFFFFFF))


def _select_topk(s_ref, thr_ref, ntiles, topk, idx_bits, rg):
    _, nslab, rows, _ = s_ref.shape
    width = nslab * LANES
    nbin = -(-topk // LANES)
    assert nbin <= nslab and rows % rg == 0
    static = isinstance(ntiles, int)

    def loop(body, init):
        if not static:
            return lax.fori_loop(0, ntiles, body, init)
        carry = init
        for t in range(ntiles):
            carry = body(t, carry)
        return carry

    for r0 in range(0, rows, rg):
        lane = lax.broadcasted_iota(I32, (rg, LANES), 1)
        slab = lambda t, j: s_ref[t, j, r0:r0 + rg, :]

        def count(pred):
            def body(t, cnt):
                for j in range(nslab):
                    cnt = cnt + jnp.where(pred(slab(t, j), t * width + j * LANES + lane), 1.0, 0.0)
                return cnt
            return jnp.sum(loop(body, jnp.zeros((rg, LANES), F32)), axis=1, keepdims=True)

        def bin_body(t, bm):
            bm = list(bm)
            for j in range(nslab):
                bm[j % nbin] = jnp.maximum(bm[j % nbin], slab(t, j))
            return tuple(bm)

        bm = loop(bin_body, tuple(jnp.full((rg, LANES), -jnp.inf, F32) for _ in range(nbin)))
        lo, hi = bm[0], bm[0]
        for j in range(1, nbin):
            lo, hi = jnp.minimum(lo, bm[j]), jnp.maximum(hi, bm[j])
        lo_f = jnp.broadcast_to(jnp.min(lo, axis=1, keepdims=True), (rg, LANES))
        hi_f = jnp.broadcast_to(jnp.max(hi, axis=1, keepdims=True), (rg, LANES))

        def unfinished(lo_k, c_lo, hi_k):
            open_ = jnp.where(c_lo == topk, 0.0, jnp.where(hi_k - 1 > lo_k, 1.0, 0.0))
            return (jnp.max(open_) > 0.5).astype(I32)

        def cond(st):
            return st[0] > 0

        def body(st):
            _, lo_k, c_lo, hi_k, c_hi = st
            mid = (lo_k >> 1) + (hi_k >> 1) + (lo_k & hi_k & 1)
            mid_f = _key_to_float(mid)
            cnt = count(lambda s, idx: s >= mid_f)
            ge = cnt >= topk
            lo_k, c_lo = jnp.where(ge, mid, lo_k), jnp.where(ge, cnt, c_lo)
            hi_k, c_hi = jnp.where(ge, hi_k, mid), jnp.where(ge, c_hi, cnt)
            return unfinished(lo_k, c_lo, hi_k), lo_k, c_lo, hi_k, c_hi

        lo_k0 = _float_to_key(lo_f)
        hi_k0 = _float_to_key(hi_f) + 1
        c_lo0 = jnp.broadcast_to(count(lambda s, idx: s >= lo_f), (rg, LANES))
        _, lo_k, c_lo, _, c_hi = lax.while_loop(
            cond, body, (unfinished(lo_k0, c_lo0, hi_k0), lo_k0, c_lo0, hi_k0, jnp.zeros((rg, LANES), F32)))
        finite = lo_k >= KEY_NEG_FLT_MAX
        thr = jnp.where(finite, _key_to_float(lo_k), -FLT_MAX)
        thr_ref[r0:r0 + rg, :] = thr

        tied = jnp.where(finite, jnp.where(c_lo > topk, 1.0, 0.0), 0.0)

        @pl.when(jnp.max(tied) > 0.5)
        def _():
            need = jnp.where(tied > 0.5, topk - c_hi, float(2 ** 30))

            def index_bit(bi, x):
                cand = x + lax.shift_left(jnp.int32(1), idx_bits - 1 - bi)
                cnt = count(lambda s, idx: jnp.where(s == thr, idx, cand) < cand)
                return jnp.where(cnt < need, cand, x)

            x = lax.fori_loop(0, idx_bits, index_bit, jnp.zeros((rg, LANES), I32))

            def fix(t_, c):
                for j in range(nslab):
                    sj = slab(t_, j)
                    drop = jnp.where(sj == thr, t_ * width + j * LANES + lane, x) > x
                    s_ref[t_, j, r0:r0 + rg, :] = jnp.where(drop, -jnp.inf, sj)
                return c

            loop(fix, 0)


def _pattn_kernel(q_ref, iq_ref, iw_ref, ikt_ref, kt_ref, v_ref, o_ref,
                  s_ref, thr_ref, wb_ref, qg_ref, iqh_ref, m_ref, l_ref, acc_ref, *, topk, idx_bits):
    bq = o_ref.shape[0]
    lb = s_ref.shape[1] * LANES
    i = pl.program_id(0)
    nkb = pl.cdiv((i + 1) * bq, lb)

    for h in range(IDX_HEADS):
        wb_ref[h] = jnp.broadcast_to(iw_ref[:, h:h + 1], (bq, LANES))
        iqh_ref[h * bq:(h + 1) * bq, :] = iq_ref[:, h * IDX_DIM:(h + 1) * IDX_DIM].astype(BF16)
    for h in range(N_HEADS):
        kvh, g = divmod(h, Q_PER_KV)
        qg_ref[kvh, g * bq:(g + 1) * bq, :] = q_ref[:, h * HEAD_DIM:(h + 1) * HEAD_DIM].astype(BF16)

    row = lax.broadcasted_iota(I32, (bq, lb), 0)
    col = lax.broadcasted_iota(I32, (bq, lb), 1)

    def scores(kb, c):
        d = jnp.dot(iqh_ref[...], ikt_ref[kb], preferred_element_type=F32)
        acc = jnp.zeros((bq, lb), F32)
        for h in range(IDX_HEADS):
            acc = acc + _tile_lanes(wb_ref[h], lb) * jnp.maximum(d[h * bq:(h + 1) * bq], 0.0)
        sc = jnp.where(kb * lb + col <= i * bq + row, acc, -jnp.inf)
        for j in range(lb // LANES):
            s_ref[kb, j] = sc[:, j * LANES:(j + 1) * LANES]
        return c

    lax.fori_loop(0, nkb, scores, 0)

    _select_topk(s_ref, thr_ref, nkb, topk, idx_bits, ROW_GROUP)
    thr_full = _tile_lanes(thr_ref[...], lb)

    m_ref[...] = jnp.full(m_ref.shape, NEG, F32)
    l_ref[...] = jnp.zeros(l_ref.shape, F32)
    acc_ref[...] = jnp.zeros(acc_ref.shape, F32)

    def attend(kb, c):
        sc = jnp.concatenate([s_ref[kb, j] for j in range(lb // LANES)], axis=1)
        bias = jnp.where(sc >= thr_full, 0.0, NEG)
        bias2 = jnp.concatenate([bias] * Q_PER_KV, axis=0)
        for kvh in range(KV_HEADS):
            s = jnp.dot(qg_ref[kvh], kt_ref[kb, kvh * HEAD_DIM:(kvh + 1) * HEAD_DIM, :],
                        preferred_element_type=F32) + bias2
            m_prev = m_ref[kvh]
            m_new = jnp.maximum(m_prev, jnp.max(s, axis=1, keepdims=True))
            alpha = jnp.exp2(m_prev - m_new)
            p = jnp.exp2(s - _tile_lanes(m_new, lb))
            psum = p[:, :LANES]
            for j in range(1, lb // LANES):
                psum = psum + p[:, j * LANES:(j + 1) * LANES]
            l_ref[kvh] = alpha * l_ref[kvh] + psum
            pair = kvh // 2
            pv = jnp.dot(p.astype(BF16), v_ref[kb, :, pair * LANES:(pair + 1) * LANES],
                         preferred_element_type=F32)
            acc_ref[kvh] = alpha * acc_ref[kvh] + pv
            m_ref[kvh] = m_new
        return c

    lax.fori_loop(0, nkb, attend, 0)

    for kvh in range(KV_HEADS):
        denom = jnp.sum(l_ref[kvh], axis=1, keepdims=True)
        off = (kvh % 2) * HEAD_DIM
        o = acc_ref[kvh][:, off:off + HEAD_DIM] / denom
        for g in range(Q_PER_KV):
            h = kvh * Q_PER_KV + g
            o_ref[:, h * HEAD_DIM:(h + 1) * HEAD_DIM] = o[g * bq:(g + 1) * bq]


def _pattn(q, iq, iw, ikt, kt, vv, topk, nqb):
    bq = BQ
    rows2 = Q_PER_KV * bq
    lb = ikt.shape[2]
    tq = nqb * bq
    nkb = pl.cdiv(tq, lb)
    assert nkb <= ikt.shape[0]
    idx_bits = int(nkb * lb).bit_length()
    resident = lambda shape: pl.BlockSpec(shape, lambda i: (0,) * len(shape), pipeline_mode=pl.Buffered(1))
    return pl.pallas_call(
        functools.partial(_pattn_kernel, topk=topk, idx_bits=idx_bits),
        grid=(nqb,),
        in_specs=[
            pl.BlockSpec((bq, q.shape[1]), lambda i: (i, 0)),
            pl.BlockSpec((bq, iq.shape[1]), lambda i: (i, 0)),
            pl.BlockSpec((bq, IDX_HEADS), lambda i: (i, 0)),
            resident(ikt.shape), resident(kt.shape), resident(vv.shape),
        ],
        out_specs=pl.BlockSpec((bq, N_HEADS * HEAD_DIM), lambda i: (i, 0)),
        out_shape=jax.ShapeDtypeStruct((tq, N_HEADS * HEAD_DIM), F32),
        scratch_shapes=[
            pltpu.VMEM((nkb, lb // LANES, bq, LANES), F32),
            pltpu.VMEM((bq, LANES), F32),
            pltpu.VMEM((IDX_HEADS, bq, LANES), F32),
            pltpu.VMEM((KV_HEADS, rows2, HEAD_DIM), BF16),
            pltpu.VMEM((IDX_HEADS * bq, IDX_DIM), BF16),
            pltpu.VMEM((KV_HEADS, rows2, LANES), F32),
            pltpu.VMEM((KV_HEADS, rows2, LANES), F32),
            pltpu.VMEM((KV_HEADS, rows2, LANES), F32),
        ],
        compiler_params=pltpu.CompilerParams(dimension_semantics=("arbitrary",), vmem_limit_bytes=VMEM_LIMIT),
        name="pattn",
    )(q, iq, iw, ikt, kt, vv)


def _sattn_kernel(pt_ref, *refs, topk, idx_bits, s_new):
    del pt_ref
    np_ = PAGES_PER_STEP
    cik = refs[0:np_]
    ck = refs[np_:2 * np_]
    cv = refs[2 * np_:3 * np_]
    iq_ref, iw_ref, qbd_ref, nik_ref, nk_ref, nv_ref, o_ref, s_ref, thr_ref, lg_ref, vst_ref = refs[3 * np_:]
    g = pl.program_id(1)
    jj = pl.program_id(2)
    ngrp, nsteps = pl.num_programs(1), pl.num_programs(2)
    nseq = o_ref.shape[0]
    ntiles, _, rows_all, page = s_ref.shape
    rows = rows_all // nseq
    width = np_ * page
    reps = rows // s_new
    row0 = pl.multiple_of(g * rows, rows)

    @pl.when(jnp.logical_and(pl.program_id(0) == 0, jnp.logical_and(g == 0, jj == 0)))
    def _():
        s_ref[ntiles - 1] = jnp.full(s_ref.shape[1:], -jnp.inf, F32)
        lg_ref[:, ntiles - 1] = jnp.zeros((nseq,) + lg_ref.shape[2:], F32)
        vst_ref[:, ntiles - 1] = jnp.zeros((nseq,) + vst_ref.shape[2:], BF16)

    def do_page(t, p, ikt, kt, vt, is_new):
        d = jnp.dot(iq_ref[...], ikt.astype(BF16), preferred_element_type=F32)
        r = jnp.maximum(d, 0.0) * iw_ref[...]
        per_q = [jnp.sum(r[q * IDX_HEADS:(q + 1) * IDX_HEADS], axis=0, keepdims=True) for q in range(s_new)]
        sc = jnp.concatenate(per_q * reps, axis=0)
        if is_new:
            qi = lax.broadcasted_iota(I32, sc.shape, 0) & (s_new - 1)
            ki = lax.broadcasted_iota(I32, sc.shape, 1)
            sc = jnp.where(ki <= qi, sc, -jnp.inf)
        s_ref[t, p, pl.ds(row0, rows), :] = sc
        lg_ref[g, t, :, p * page:(p + 1) * page] = jnp.dot(qbd_ref[...], kt.astype(BF16),
                                                           preferred_element_type=F32)
        vst_ref[g, t, :, p * page:(p + 1) * page] = vt.astype(BF16)

    for p in range(np_):
        do_page(jj, p, cik[p][...], ck[p][...], cv[p][...], False)

    @pl.when(jj == nsteps - 1)
    def _():
        do_page(ntiles - 1, 0, nik_ref[...], nk_ref[...], nv_ref[...], True)

    @pl.when(jnp.logical_and(g == ngrp - 1, jj == nsteps - 1))
    def _():
        _select_topk(s_ref, thr_ref, ntiles, topk, idx_bits, rows_all)
        nvreg = lg_ref.shape[2] // rows
        for q in range(nseq):
            thr_full = _tile_lanes(thr_ref[q * rows:(q + 1) * rows, :], width)

            def masked(t):
                sc = jnp.concatenate([s_ref[t, j, q * rows:(q + 1) * rows, :] for j in range(np_)], axis=1)
                bias = jnp.where(sc >= thr_full, 0.0, NEG)
                return lg_ref[q, t] + jnp.concatenate([bias] * nvreg, axis=0)

            m = jnp.max(masked(0), axis=1, keepdims=True)
            for t in range(1, ntiles):
                m = jnp.maximum(m, jnp.max(masked(t), axis=1, keepdims=True))
            l = jnp.zeros((lg_ref.shape[2], 1), F32)
            acc = jnp.zeros(o_ref.shape[1:], F32)
            for t in range(ntiles):
                p = jnp.exp2(masked(t) - m)
                l = l + jnp.sum(p, axis=1, keepdims=True)
                acc = acc + lax.dot_general(p.astype(BF16), vst_ref[q, t], (((1,), (1,)), ((), ())),
                                            preferred_element_type=F32)
            o_ref[q] = acc / l


def _sattn(page_table, cache_ikt, cache_kt, cache_vt, iq_s, iw_rep, qbd, nikt, nkt, nvt, topk):
    db, pages = page_table.shape
    page = cache_ikt.shape[2]
    assert page == LANES and pages % PAGES_PER_STEP == 0
    s_new = iq_s.shape[1] // IDX_HEADS
    rows = SUBLANES
    assert rows % s_new == 0 and s_new & (s_new - 1) == 0
    nseq = math.gcd(db, SEQS_PER_GROUP)
    nrow = qbd.shape[1]
    dkv = cache_kt.shape[1]
    nsteps = pages // PAGES_PER_STEP
    width = PAGES_PER_STEP * page
    idx_bits = int((nsteps + 1) * width).bit_length()

    def page_spec(depth, p):
        return pl.BlockSpec(
            (None, depth, page),
            lambda bo, g, jj, pt, p=p: (pt[(bo * nseq + g) * pages + jj * PAGES_PER_STEP + p], 0, 0))

    per_seq = lambda shape: pl.BlockSpec((None,) + shape, lambda bo, g, jj, pt: (bo * nseq + g, 0, 0))
    in_specs = ([page_spec(IDX_DIM, p) for p in range(PAGES_PER_STEP)]
                + [page_spec(dkv, p) for p in range(PAGES_PER_STEP)]
                + [page_spec(dkv, p) for p in range(PAGES_PER_STEP)]
                + [per_seq(iq_s.shape[1:]), per_seq(iw_rep.shape[1:]), per_seq(qbd.shape[1:]),
                   per_seq(nikt.shape[1:]), per_seq(nkt.shape[1:]), per_seq(nvt.shape[1:])])
    grid_spec = pltpu.PrefetchScalarGridSpec(
        num_scalar_prefetch=1,
        grid=(db // nseq, nseq, nsteps),
        in_specs=in_specs,
        out_specs=pl.BlockSpec((nseq, nrow, dkv), lambda bo, g, jj, pt: (bo, 0, 0)),
        scratch_shapes=[
            pltpu.VMEM((nsteps + 1, PAGES_PER_STEP, nseq * rows, page), F32),
            pltpu.VMEM((nseq * rows, LANES), F32),
            pltpu.VMEM((nseq, nsteps + 1, nrow, width), F32),
            pltpu.VMEM((nseq, nsteps + 1, dkv, width), BF16),
        ],
    )
    args = ([cache_ikt] * PAGES_PER_STEP + [cache_kt] * PAGES_PER_STEP + [cache_vt] * PAGES_PER_STEP
            + [iq_s, iw_rep, qbd, nikt, nkt, nvt])
    return pl.pallas_call(
        functools.partial(_sattn_kernel, topk=topk, idx_bits=idx_bits, s_new=s_new),
        grid_spec=grid_spec,
        out_shape=jax.ShapeDtypeStruct((db, nrow, dkv), F32),
        compiler_params=pltpu.CompilerParams(dimension_semantics=("arbitrary", "arbitrary", "arbitrary"),
                                             vmem_limit_bytes=VMEM_LIMIT),
        name="sattn",
    )(page_table.reshape(-1), *args)


def _merge_kernel(x_ref, attn_ref, z_ref, z1_ref, z2_ref, cb_ref, ga_ref, gb_ref,
                  wc_ref, wa_ref, wb_ref, wo_ref, g_ref, b_ref, wr_ref, br_ref,
                  h_ref, comb_ref, *, alpha):
    wc = wc_ref[...]
    y = wc[0:1] * z2_ref[...] + wc[1:2] * z1_ref[...] + wc[2:3] * z_ref[...]
    conv_o = cb_ref[...] * y
    a = jnp.dot(attn_ref[...].astype(BF16), wa_ref[...], preferred_element_type=F32)
    b = jnp.dot(conv_o.astype(BF16), wb_ref[...], preferred_element_type=F32)
    sig = lambda u: 1.0 / (1.0 + jnp.exp(-u))
    mixed = sig(ga_ref[...]) * a + sig(gb_ref[...]) * b
    mix = jnp.dot(mixed.astype(BF16), wo_ref[...], preferred_element_type=F32)
    h = _layer_norm(alpha * x_ref[...] + mix, g_ref[...], b_ref[...])
    h_ref[...] = h

    logits = jnp.dot(h, wr_ref[...], preferred_element_type=F32, precision=lax.Precision.HIGHEST) + br_ref[...]
    lane_i = lax.broadcasted_iota(I32, logits.shape, 1)
    lane = lane_i.astype(F32)
    first_where = lambda cond: jnp.min(jnp.where(cond, lane, float(LANES)), axis=1, keepdims=True)
    is_g = lane_i < N_GROUPS
    gl = jnp.where(is_g, logits, -jnp.inf)
    gmax = jnp.max(gl, axis=1, keepdims=True)
    g_sel = first_where(gl == gmax)
    g_p = 1.0 / jnp.sum(jnp.where(is_g, jnp.exp(logits - gmax), 0.0), axis=1, keepdims=True)
    grp = jnp.where((lane_i >= N_GROUPS) & (lane_i < N_GROUPS + N_EXPERTS),
                    lax.shift_right_arithmetic(lane_i - N_GROUPS, jnp.int32(int(math.log2(EXPERTS_PER_GROUP)))),
                    -1).astype(F32)
    in_grp = grp == g_sel
    e1 = jnp.where(in_grp, logits, -jnp.inf)
    max1 = jnp.max(e1, axis=1, keepdims=True)
    i1 = first_where(e1 == max1)
    e2 = jnp.where(lane == i1, -jnp.inf, e1)
    max2 = jnp.max(e2, axis=1, keepdims=True)
    i2 = first_where(e2 == max2)
    den = jnp.sum(jnp.where(in_grp, jnp.exp(logits - max1), 0.0), axis=1, keepdims=True)
    p1 = 1.0 / den
    p2 = jnp.exp(max2 - max1) / den
    tot = p1 + p2
    comb_ref[...] = jnp.where(lane == i1, p1 / tot * g_p, 0.0) + jnp.where(lane == i2, p2 / tot * g_p, 0.0)


def _merge(x, attn_o, z, z1, z2, cb, ga, gb, wc, wa, wb, wo, g, b, wr, br, alpha):
    tp, d = x.shape
    row = lambda width: pl.BlockSpec((TM, width), lambda i: (i, 0))
    const = lambda a: pl.BlockSpec(a.shape, lambda i: (0, 0), pipeline_mode=pl.Buffered(1))
    dc = z.shape[1]
    return pl.pallas_call(
        functools.partial(_merge_kernel, alpha=alpha),
        grid=(tp // TM,),
        in_specs=[row(d), row(attn_o.shape[1]), row(dc), row(dc), row(dc), row(dc), row(d), row(d),
                  const(wc), const(wa), const(wb), const(wo), const(g), const(b), const(wr), const(br)],
        out_specs=[row(d), row(LANES)],
        out_shape=[jax.ShapeDtypeStruct((tp, d), F32), jax.ShapeDtypeStruct((tp, LANES), F32)],
        compiler_params=pltpu.CompilerParams(dimension_semantics=("arbitrary",), vmem_limit_bytes=VMEM_LIMIT),
        name="merge",
    )(x, attn_o, z, z1, z2, cb, ga, gb, wc, wa, wb, wo, g, b, wr, br)


def _moe_kernel(h_ref, comb_ref, wgu_ref, wd_ref, g_ref, b_ref, o_ref, acc_ref, hb_ref, *, alpha):
    e = pl.program_id(1)
    de = wd_ref.shape[0]

    @pl.when(e == 0)
    def _():
        acc_ref[...] = jnp.zeros(acc_ref.shape, F32)
        hb_ref[...] = h_ref[...].astype(BF16)

    gu = jnp.dot(hb_ref[...], wgu_ref[...], preferred_element_type=F32)
    hg = gu[:, :de]
    hu = gu[:, de:]
    lane = lax.broadcasted_iota(I32, comb_ref.shape, 1)
    c = jnp.sum(jnp.where(lane == e + N_GROUPS, comb_ref[...], 0.0), axis=1, keepdims=True)
    act = hg * (1.0 / (1.0 + jnp.exp(-hg))) * hu * c
    acc_ref[...] += jnp.dot(act.astype(BF16), wd_ref[...], preferred_element_type=F32)

    @pl.when(e == pl.num_programs(1) - 1)
    def _():
        o_ref[...] = _layer_norm(alpha * h_ref[...] + acc_ref[...], g_ref[...], b_ref[...])


def _moe(h, comb, wgu, wd, g, b, alpha):
    tp, d = h.shape
    ne, _, de2 = wgu.shape
    de = wd.shape[1]
    return pl.pallas_call(
        functools.partial(_moe_kernel, alpha=alpha),
        grid=(tp // TM_MOE, ne),
        in_specs=[
            pl.BlockSpec((TM_MOE, d), lambda i, e: (i, 0)),
            pl.BlockSpec((TM_MOE, LANES), lambda i, e: (i, 0)),
            pl.BlockSpec((None, d, de2), lambda i, e: (e, 0, 0)),
            pl.BlockSpec((None, de, d), lambda i, e: (e, 0, 0)),
            pl.BlockSpec((1, d), lambda i, e: (0, 0)),
            pl.BlockSpec((1, d), lambda i, e: (0, 0)),
        ],
        out_specs=pl.BlockSpec((TM_MOE, d), lambda i, e: (i, 0)),
        out_shape=jax.ShapeDtypeStruct((tp, d), F32),
        scratch_shapes=[pltpu.VMEM((TM_MOE, d), F32), pltpu.VMEM((TM_MOE, d), BF16)],
        compiler_params=pltpu.CompilerParams(dimension_semantics=("arbitrary", "arbitrary"),
                                             vmem_limit_bytes=VMEM_LIMIT),
        name="moe",
    )(h, comb, wgu, wd, g, b)


def _rope_tables(pos):
    half = HEAD_DIM // 2
    inv = jnp.power(jnp.float32(ROPE_THETA), -jnp.arange(half, dtype=F32) * 2.0 / HEAD_DIM)
    ang = pos.astype(F32)[:, None] * inv[None, :]
    cos, sin = jnp.cos(ang), jnp.sin(ang)
    reps = LANES // HEAD_DIM
    return (jnp.concatenate([cos, cos] * reps, axis=1), jnp.concatenate([-sin, sin] * reps, axis=1))


def _pad_rows(a, rows):
    return jnp.pad(a, [(0, rows - a.shape[0])] + [(0, 0)] * (a.ndim - 1))


def kernel(x_prompt, x_sample, cache_k, cache_v, cache_idx_k, state_conv, page_table, meta_tokens, w_in, b_in, w_conv, w_attn_up, w_conv_out, w_o, ln1_g, ln1_b, w_group, b_group, w_expert_router, b_expert_router, w_gate, w_up, w_down, ln2_g, ln2_b):
    bsz, s_p, d = x_prompt.shape
    db, s_s, _ = x_sample.shape
    depth = w_in.shape[0]
    assert bsz == 1, "prompt group is served one sequence at a time"
    assert s_s >= CONV_WIDTH - 1
    n_phys, page = cache_k.shape[1], cache_k.shape[2]
    pages = page_table.shape[1]
    past_len = pages * page
    t_p = s_p + N_META
    t_s = db * s_s
    t_all = t_p + t_s
    topk_p = min(TOPK_MAX, t_p // 4)
    topk_s = min(TOPK_MAX, (past_len + s_s) // 4)
    alpha = (2 * depth) ** 0.25

    d_attn = N_HEADS * HEAD_DIM
    d_kv = KV_HEADS * HEAD_DIM
    d_idx = IDX_HEADS * IDX_DIM
    d_conv = w_conv.shape[2]
    segs, offs, _ = _proj_layout(d_attn, d_kv, d_idx, d_conv, d)
    widths = {name: width for name, width, _ in segs}
    ref_order = ["q", "k", "v", "iq", "iw", "ik", "cu", "cb", "cc", "ga", "gb"]
    ref_starts = np.concatenate([[0], np.cumsum([widths[nm] for nm in ref_order])])

    tq = -(-t_p // BQ) * BQ
    tk = -(-tq // LB) * LB
    tile = max(TM, TM_MOE, LB)
    tp = -(-max(t_all, tk) // tile) * tile
    nqb = tq // BQ

    pos = jnp.concatenate([jnp.arange(t_p, dtype=jnp.int32),
                           jnp.tile(past_len + jnp.arange(s_s, dtype=jnp.int32), db),
                           jnp.zeros((tp - t_all,), jnp.int32)])
    cos_t, sin_t = _rope_tables(pos)

    h_p = jnp.concatenate([meta_tokens.astype(x_prompt.dtype), x_prompt[0]], axis=0)
    h = _pad_rows(jnp.concatenate([h_p, x_sample.reshape(t_s, d)], axis=0), tp)

    q_scale = HEAD_DIM ** -0.5 * math.log2(math.e)

    outs = {k: [] for k in ("kp", "vp", "ikp", "cp", "ks", "vs", "iks", "cs")}
    for l in range(depth):
        pieces_w, pieces_b = [], []
        for nm, width, padded in segs:
            a = int(ref_starts[ref_order.index(nm)])
            pieces_w.append(jnp.pad(w_in[l][:, a:a + width], [(0, 0), (0, padded - width)]))
            pieces_b.append(jnp.pad(b_in[l][a:a + width], [(0, padded - width)]))
        w_p = jnp.concatenate(pieces_w, axis=1).astype(BF16)
        b_p = jnp.concatenate(pieces_b)[None, :]

        (q_r, iq_r, k_r, v_r, ik_r, kt_b, ikt_b, v_b, iw_r, z, cb, ga, gb) = _proj(
            h, w_p, b_p, cos_t, sin_t, offs, widths, q_scale)

        attn_p = _pattn(q_r, iq_r, iw_r, ikt_b, kt_b, v_b.reshape(tp // LB, LB, d_kv), topk_p, nqb)

        sl = slice(t_p, t_all)
        iq_s = iq_r[sl].astype(BF16).reshape(db, s_s * IDX_HEADS, IDX_DIM)
        iw_rep = jnp.broadcast_to(iw_r[sl].reshape(db, s_s * IDX_HEADS, 1), (db, s_s * IDX_HEADS, LANES))
        q_s = q_r[sl].astype(BF16).reshape(db, s_s, KV_HEADS, Q_PER_KV, HEAD_DIM)
        q_s = q_s.transpose(0, 2, 3, 1, 4)
        eye = jnp.eye(KV_HEADS, dtype=BF16)
        qbd = (q_s[:, :, :, :, None, :] * eye[None, :, None, None, :, None])
        qbd = qbd.reshape(db, KV_HEADS * Q_PER_KV * s_s, d_kv)
        new_page_t = lambda a: jnp.pad(a[sl].reshape(db, s_s, a.shape[1]).transpose(0, 2, 1),
                                       [(0, 0), (0, 0), (0, page - s_s)])
        cache_ikt = cache_idx_k[l].transpose(0, 2, 1)
        cache_kt = cache_k[l].transpose(0, 2, 3, 1).reshape(n_phys, d_kv, page)
        cache_vt = cache_v[l].transpose(0, 2, 3, 1).reshape(n_phys, d_kv, page)
        r_s = _sattn(page_table, cache_ikt, cache_kt, cache_vt, iq_s, iw_rep, qbd,
                     new_page_t(ik_r), new_page_t(k_r), new_page_t(v_r), topk_s)
        r_s = r_s.reshape(db, KV_HEADS, Q_PER_KV, s_s, KV_HEADS, HEAD_DIM)
        attn_s = jnp.stack([r_s[:, kvh, :, :, kvh, :] for kvh in range(KV_HEADS)], axis=1)
        attn_s = attn_s.transpose(0, 3, 1, 2, 4).reshape(t_s, d_attn)

        attn_o = _pad_rows(jnp.concatenate([attn_p[:t_p], attn_s], axis=0), tp)

        zp = jnp.concatenate([jnp.zeros((CONV_WIDTH - 1, d_conv), F32), z[:t_p]], axis=0)
        zs = jnp.concatenate([state_conv[l].astype(F32), z[sl].reshape(db, s_s, d_conv)], axis=1)
        z1 = _pad_rows(jnp.concatenate([zp[1:1 + t_p], zs[:, 1:1 + s_s].reshape(t_s, d_conv)], axis=0), tp)
        z2 = _pad_rows(jnp.concatenate([zp[0:t_p], zs[:, 0:s_s].reshape(t_s, d_conv)], axis=0), tp)

        wr = jnp.pad(jnp.concatenate([w_group[l], w_expert_router[l]], axis=1),
                     [(0, 0), (0, LANES - N_GROUPS - N_EXPERTS)])
        br = jnp.pad(jnp.concatenate([b_group[l], b_expert_router[l]]), [(0, LANES - N_GROUPS - N_EXPERTS)])[None]
        h1, comb = _merge(h, attn_o, z, z1, z2, cb, ga, gb, w_conv[l],
                          w_attn_up[l].astype(BF16), w_conv_out[l].astype(BF16), w_o[l].astype(BF16),
                          ln1_g[l][None], ln1_b[l][None], wr, br, alpha)

        wgu = jnp.concatenate([w_gate[l], w_up[l]], axis=2).astype(BF16)
        h = _moe(h1, comb, wgu, w_down[l].astype(BF16), ln2_g[l][None], ln2_b[l][None], alpha)

        outs["kp"].append(k_r[:t_p].reshape(bsz, t_p, KV_HEADS, HEAD_DIM))
        outs["vp"].append(v_r[:t_p].reshape(bsz, t_p, KV_HEADS, HEAD_DIM))
        outs["ikp"].append(ik_r[:t_p].reshape(bsz, t_p, IDX_DIM))
        outs["cp"].append(zp[-(CONV_WIDTH - 1):].reshape(bsz, CONV_WIDTH - 1, d_conv))
        outs["ks"].append(k_r[sl].reshape(db, s_s, KV_HEADS, HEAD_DIM))
        outs["vs"].append(v_r[sl].reshape(db, s_s, KV_HEADS, HEAD_DIM))
        outs["iks"].append(ik_r[sl].reshape(db, s_s, IDX_DIM))
        outs["cs"].append(zs[:, -(CONV_WIDTH - 1):])

    y_prompt = h[N_META:t_p].reshape(bsz, s_p, d)
    y_sample = h[t_p:t_all].reshape(db, s_s, d)
    st = lambda k: jnp.stack(outs[k])
    return (y_prompt, y_sample, st("kp"), st("vp"), st("ikp"), st("cp"), st("ks"), st("vs"), st("iks"), st("cs"))
```

```python
import functools
import math

import numpy as np
import jax
import jax.numpy as jnp
from jax import lax
from jax.experimental import pallas as pl
from jax.experimental.pallas import tpu as pltpu

F32 = jnp.float32
BF16 = jnp.bfloat16
I32 = jnp.int32

N_META = 16
N_HEADS = 8
HEAD_DIM = 64
KV_HEADS = 4
Q_PER_KV = N_HEADS // KV_HEADS
IDX_HEADS = 8
IDX_DIM = 64
TOPK_MAX = 256
CONV_WIDTH = 3
N_GROUPS = 4
EXPERTS_PER_GROUP = 4
N_EXPERTS = N_GROUPS * EXPERTS_PER_GROUP
ROPE_THETA = 10000.0
LN_EPS = 1e-5
IDX_W_SCALE = (IDX_HEADS ** -0.5) * (IDX_DIM ** -0.5)

LANES = 128
SUBLANES = 8
INT_MIN = -(2 ** 31)
KEY_NEG_FLT_MAX = INT_MIN + (1 << 23)
FLT_MAX = float(np.finfo(np.float32).max)
NEG = -1e30
VMEM_LIMIT = 56 * 1024 * 1024

TM = 256
TM_MOE = 1024
BQ = 256
LB = 512
ROW_GROUP = 128
PAGES_PER_STEP = 16
SEQS_PER_GROUP = 4


def _tile_lanes(x, n):
    reps = n // x.shape[1]
    return x if reps == 1 else jnp.concatenate([x] * reps, axis=1)


def _layer_norm(x, g, b):
    mu = jnp.mean(x, axis=-1, keepdims=True)
    xc = x - mu
    var = jnp.mean(xc * xc, axis=-1, keepdims=True)
    return xc * lax.rsqrt(var + LN_EPS) * g + b


def _proj_layout(d_attn, d_kv, d_idx, d_conv, d_model):
    segs = [("q", d_attn, d_attn), ("k", d_kv, d_kv), ("v", d_kv, d_kv), ("iq", d_idx, d_idx),
            ("iw", IDX_HEADS, LANES), ("ik", IDX_DIM, LANES),
            ("cu", d_conv, d_conv), ("cb", d_conv, d_conv), ("cc", d_conv, d_conv),
            ("ga", d_model, d_model), ("gb", d_model, d_model)]
    offs, o = {}, 0
    for name, _, padded in segs:
        offs[name] = (o, padded)
        o += padded
    return segs, offs, o


def _proj_kernel(x_ref, w_ref, b_ref, cos_ref, sin_ref,
                 q_ref, iq_ref, k_ref, v_ref, ik_ref, kt_ref, ikt_ref, vb_ref, iw_ref, z_ref, cb_ref, ga_ref, gb_ref,
                 *, offs, q_scale):
    xb = x_ref[...].astype(BF16)

    def seg(name):
        a, n = offs[name]
        return jnp.dot(xb, w_ref[:, a:a + n], preferred_element_type=F32) + b_ref[:, a:a + n]

    cos = cos_ref[...]
    sin = sin_ref[...]

    def rope(y):
        n = y.shape[1]
        lane = lax.broadcasted_iota(I32, y.shape, 1)
        first_half = (lane & (HEAD_DIM - 1)) < (HEAD_DIM // 2)
        swapped = jnp.where(first_half, pltpu.roll(y, n - HEAD_DIM // 2, 1), pltpu.roll(y, HEAD_DIM // 2, 1))
        return y * _tile_lanes(cos, n) + swapped * _tile_lanes(sin, n)

    q_ref[...] = rope(seg("q")) * q_scale
    iq_ref[...] = rope(seg("iq"))
    k = rope(seg("k"))
    k_ref[...] = k
    kt_ref[...] = k.T.astype(BF16)
    v = seg("v")
    v_ref[...] = v
    vb_ref[...] = v.astype(BF16)
    ik = rope(seg("ik"))
    ik_ref[...] = ik[:, :IDX_DIM]
    ikt_ref[...] = ik.T[:IDX_DIM].astype(BF16)
    iw_ref[...] = seg("iw")[:, :IDX_HEADS] * IDX_W_SCALE
    z_ref[...] = seg("cc") * seg("cu")
    cb_ref[...] = seg("cb")
    ga_ref[...] = seg("ga")
    gb_ref[...] = seg("gb")


def _proj(x, w, b, cos, sin, offs, widths, q_scale):
    tp, d = x.shape
    n = w.shape[1]
    tm = LB
    nb = tp // tm
    row = lambda width: pl.BlockSpec((tm, width), lambda i: (i, 0))
    blk = lambda depth: pl.BlockSpec((None, depth, tm), lambda i: (i, 0, 0))
    const = lambda shape: pl.BlockSpec(shape, lambda i: (0, 0), pipeline_mode=pl.Buffered(1))
    dq, dkv, dc, dm = widths["q"], widths["k"], widths["cu"], widths["ga"]
    f32 = lambda width: jax.ShapeDtypeStruct((tp, width), F32)
    out = [(row(dq), f32(dq)), (row(widths["iq"]), f32(widths["iq"])),
           (row(dkv), f32(dkv)), (row(dkv), f32(dkv)), (row(IDX_DIM), f32(IDX_DIM)),
           (blk(dkv), jax.ShapeDtypeStruct((nb, dkv, tm), BF16)),
           (blk(IDX_DIM), jax.ShapeDtypeStruct((nb, IDX_DIM, tm), BF16)),
           (row(dkv), jax.ShapeDtypeStruct((tp, dkv), BF16)),
           (row(IDX_HEADS), f32(IDX_HEADS)), (row(dc), f32(dc)), (row(dc), f32(dc)),
           (row(dm), f32(dm)), (row(dm), f32(dm))]
    return pl.pallas_call(
        functools.partial(_proj_kernel, offs=offs, q_scale=q_scale),
        grid=(nb,),
        in_specs=[row(d), const((d, n)), const((1, n)), row(LANES), row(LANES)],
        out_specs=[o[0] for o in out],
        out_shape=[o[1] for o in out],
        compiler_params=pltpu.CompilerParams(dimension_semantics=("arbitrary",), vmem_limit_bytes=VMEM_LIMIT),
        name="proj",
    )(x, w, b, cos, sin)


def _key_to_float(c):
    bits = jnp.where(c >= 0, c, c ^ jnp.int32(0x7FFFFFFF))
    return lax.bitcast_convert_type(bits, F32)


def _float_to_key(x):
    bits = lax.bitcast_convert_type(x, I32)
    return jnp.where(bits >= 0, bits, bits ^ jnp.int32(0x7FFFFFFF))


def _select_topk(s_ref, thr_ref, ntiles, topk, idx_bits, rg):
    _, nslab, rows, _ = s_ref.shape
    width = nslab * LANES
    nbin = -(-topk // LANES)
    assert nbin <= nslab and rows % rg == 0
    static = isinstance(ntiles, int)

    def loop(body, init):
        if not static:
            return lax.fori_loop(0, ntiles, body, init)
        carry = init
        for t in range(ntiles):
            carry = body(t, carry)
        return carry

    for r0 in range(0, rows, rg):
        lane = lax.broadcasted_iota(I32, (rg, LANES), 1)
        slab = lambda t, j: s_ref[t, j, r0:r0 + rg, :]

        def count(pred):
            def body(t, cnt):
                for j in range(nslab):
                    cnt = cnt + jnp.where(pred(slab(t, j), t * width + j * LANES + lane), 1.0, 0.0)
                return cnt
            return jnp.sum(loop(body, jnp.zeros((rg, LANES), F32)), axis=1, keepdims=True)

        def bin_body(t, bm):
            bm = list(bm)
            for j in range(nslab):
                bm[j % nbin] = jnp.maximum(bm[j % nbin], slab(t, j))
            return tuple(bm)

        bm = loop(bin_body, tuple(jnp.full((rg, LANES), -jnp.inf, F32) for _ in range(nbin)))
        lo, hi = bm[0], bm[0]
        for j in range(1, nbin):
            lo, hi = jnp.minimum(lo, bm[j]), jnp.maximum(hi, bm[j])
        lo_f = jnp.broadcast_to(jnp.min(lo, axis=1, keepdims=True), (rg, LANES))
        hi_f = jnp.broadcast_to(jnp.max(hi, axis=1, keepdims=True), (rg, LANES))

        def unfinished(lo_k, c_lo, hi_k):
            open_ = jnp.where(c_lo == topk, 0.0, jnp.where(hi_k - 1 > lo_k, 1.0, 0.0))
            return (jnp.max(open_) > 0.5).astype(I32)

        def cond(st):
            return st[0] > 0

        def body(st):
            _, lo_k, c_lo, hi_k, c_hi = st
            mid = (lo_k >> 1) + (hi_k >> 1) + (lo_k & hi_k & 1)
            mid_f = _key_to_float(mid)
            cnt = count(lambda s, idx: s >= mid_f)
            ge = cnt >= topk
            lo_k, c_lo = jnp.where(ge, mid, lo_k), jnp.where(ge, cnt, c_lo)
            hi_k, c_hi = jnp.where(ge, hi_k, mid), jnp.where(ge, c_hi, cnt)
            return unfinished(lo_k, c_lo, hi_k), lo_k, c_lo, hi_k, c_hi

        lo_k0 = _float_to_key(lo_f)
        hi_k0 = _float_to_key(hi_f) + 1
        c_lo0 = jnp.broadcast_to(count(lambda s, idx: s >= lo_f), (rg, LANES))
        _, lo_k, c_lo, _, c_hi = lax.while_loop(
            cond, body, (unfinished(lo_k0, c_lo0, hi_k0), lo_k0, c_lo0, hi_k0, jnp.zeros((rg, LANES), F32)))
        finite = lo_k >= KEY_NEG_FLT_MAX
        thr = jnp.where(finite, _key_to_float(lo_k), -FLT_MAX)
        thr_ref[r0:r0 + rg, :] = thr

        tied = jnp.where(finite, jnp.where(c_lo > topk, 1.0, 0.0), 0.0)

        @pl.when(jnp.max(tied) > 0.5)
        def _():
            need = jnp.where(tied > 0.5, topk - c_hi, float(2 ** 30))

            def index_bit(bi, x):
                cand = x + lax.shift_left(jnp.int32(1), idx_bits - 1 - bi)
                cnt = count(lambda s, idx: jnp.where(s == thr, idx, cand) < cand)
                return jnp.where(cnt < need, cand, x)

            x = lax.fori_loop(0, idx_bits, index_bit, jnp.zeros((rg, LANES), I32))

            def fix(t_, c):
                for j in range(nslab):
                    sj = slab(t_, j)
                    drop = jnp.where(sj == thr, t_ * width + j * LANES + lane, x) > x
                    s_ref[t_, j, r0:r0 + rg, :] = jnp.where(drop, -jnp.inf, sj)
                return c

            loop(fix, 0)


def _pattn_kernel(q_ref, iq_ref, iw_ref, ikt_ref, kt_ref, v_ref, o_ref,
                  s_ref, thr_ref, wb_ref, qg_ref, iqh_ref, m_ref, l_ref, acc_ref, *, topk, idx_bits):
    bq = o_ref.shape[0]
    lb = s_ref.shape[1] * LANES
    i = pl.program_id(0)
    nkb = pl.cdiv((i + 1) * bq, lb)

    for h in range(IDX_HEADS):
        wb_ref[h] = jnp.broadcast_to(iw_ref[:, h:h + 1], (bq, LANES))
        iqh_ref[h * bq:(h + 1) * bq, :] = iq_ref[:, h * IDX_DIM:(h + 1) * IDX_DIM].astype(BF16)
    for h in range(N_HEADS):
        kvh, g = divmod(h, Q_PER_KV)
        qg_ref[kvh, g * bq:(g + 1) * bq, :] = q_ref[:, h * HEAD_DIM:(h + 1) * HEAD_DIM].astype(BF16)

    row = lax.broadcasted_iota(I32, (bq, lb), 0)
    col = lax.broadcasted_iota(I32, (bq, lb), 1)

    def scores(kb, c):
        d = jnp.dot(iqh_ref[...], ikt_ref[kb], preferred_element_type=F32)
        acc = jnp.zeros((bq, lb), F32)
        for h in range(IDX_HEADS):
            acc = acc + _tile_lanes(wb_ref[h], lb) * jnp.maximum(d[h * bq:(h + 1) * bq], 0.0)
        sc = jnp.where(kb * lb + col <= i * bq + row, acc, -jnp.inf)
        for j in range(lb // LANES):
            s_ref[kb, j] = sc[:, j * LANES:(j + 1) * LANES]
        return c

    lax.fori_loop(0, nkb, scores, 0)

    _select_topk(s_ref, thr_ref, nkb, topk, idx_bits, ROW_GROUP)
    thr_full = _tile_lanes(thr_ref[...], lb)

    m_ref[...] = jnp.full(m_ref.shape, NEG, F32)
    l_ref[...] = jnp.zeros(l_ref.shape, F32)
    acc_ref[...] = jnp.zeros(acc_ref.shape, F32)

    def attend(kb, c):
        sc = jnp.concatenate([s_ref[kb, j] for j in range(lb // LANES)], axis=1)
        bias = jnp.where(sc >= thr_full, 0.0, NEG)
        bias2 = jnp.concatenate([bias] * Q_PER_KV, axis=0)
        for kvh in range(KV_HEADS):
            s = jnp.dot(qg_ref[kvh], kt_ref[kb, kvh * HEAD_DIM:(kvh + 1) * HEAD_DIM, :],
                        preferred_element_type=F32) + bias2
            m_prev = m_ref[kvh]
            m_new = jnp.maximum(m_prev, jnp.max(s, axis=1, keepdims=True))
            alpha = jnp.exp2(m_prev - m_new)
            p = jnp.exp2(s - _tile_lanes(m_new, lb))
            psum = p[:, :LANES]
            for j in range(1, lb // LANES):
                psum = psum + p[:, j * LANES:(j + 1) * LANES]
            l_ref[kvh] = alpha * l_ref[kvh] + psum
            pair = kvh // 2
            pv = jnp.dot(p.astype(BF16), v_ref[kb, :, pair * LANES:(pair + 1) * LANES],
                         preferred_element_type=F32)
            acc_ref[kvh] = alpha * acc_ref[kvh] + pv
            m_ref[kvh] = m_new
        return c

    lax.fori_loop(0, nkb, attend, 0)

    for kvh in range(KV_HEADS):
        denom = jnp.sum(l_ref[kvh], axis=1, keepdims=True)
        off = (kvh % 2) * HEAD_DIM
        o = acc_ref[kvh][:, off:off + HEAD_DIM] / denom
        for g in range(Q_PER_KV):
            h = kvh * Q_PER_KV + g
            o_ref[:, h * HEAD_DIM:(h + 1) * HEAD_DIM] = o[g * bq:(g + 1) * bq]


def _pattn(q, iq, iw, ikt, kt, vv, topk, nqb):
    bq = BQ
    rows2 = Q_PER_KV * bq
    lb = ikt.shape[2]
    tq = nqb * bq
    nkb = pl.cdiv(tq, lb)
    assert nkb <= ikt.shape[0]
    idx_bits = int(nkb * lb).bit_length()
    resident = lambda shape: pl.BlockSpec(shape, lambda i: (0,) * len(shape), pipeline_mode=pl.Buffered(1))
    return pl.pallas_call(
        functools.partial(_pattn_kernel, topk=topk, idx_bits=idx_bits),
        grid=(nqb,),
        in_specs=[
            pl.BlockSpec((bq, q.shape[1]), lambda i: (i, 0)),
            pl.BlockSpec((bq, iq.shape[1]), lambda i: (i, 0)),
            pl.BlockSpec((bq, IDX_HEADS), lambda i: (i, 0)),
            resident(ikt.shape), resident(kt.shape), resident(vv.shape),
        ],
        out_specs=pl.BlockSpec((bq, N_HEADS * HEAD_DIM), lambda i: (i, 0)),
        out_shape=jax.ShapeDtypeStruct((tq, N_HEADS * HEAD_DIM), F32),
        scratch_shapes=[
            pltpu.VMEM((nkb, lb // LANES, bq, LANES), F32),
            pltpu.VMEM((bq, LANES), F32),
            pltpu.VMEM((IDX_HEADS, bq, LANES), F32),
            pltpu.VMEM((KV_HEADS, rows2, HEAD_DIM), BF16),
            pltpu.VMEM((IDX_HEADS * bq, IDX_DIM), BF16),
            pltpu.VMEM((KV_HEADS, rows2, LANES), F32),
            pltpu.VMEM((KV_HEADS, rows2, LANES), F32),
            pltpu.VMEM((KV_HEADS, rows2, LANES), F32),
        ],
        compiler_params=pltpu.CompilerParams(dimension_semantics=("arbitrary",), vmem_limit_bytes=VMEM_LIMIT),
        name="pattn",
    )(q, iq, iw, ikt, kt, vv)


def _sattn_kernel(pt_ref, *refs, topk, idx_bits, s_new):
    del pt_ref
    np_ = PAGES_PER_STEP
    cik = refs[0:np_]
    ck = refs[np_:2 * np_]
    cv = refs[2 * np_:3 * np_]
    iq_ref, iw_ref, qbd_ref, nik_ref, nk_ref, nv_ref, o_ref, s_ref, thr_ref, lg_ref, vst_ref = refs[3 * np_:]
    g = pl.program_id(1)
    jj = pl.program_id(2)
    ngrp, nsteps = pl.num_programs(1), pl.num_programs(2)
    nseq = o_ref.shape[0]
    ntiles, _, rows_all, page = s_ref.shape
    rows = rows_all // nseq
    width = np_ * page
    reps = rows // s_new
    row0 = pl.multiple_of(g * rows, rows)

    @pl.when(jnp.logical_and(pl.program_id(0) == 0, jnp.logical_and(g == 0, jj == 0)))
    def _():
        s_ref[ntiles - 1] = jnp.full(s_ref.shape[1:], -jnp.inf, F32)
        lg_ref[:, ntiles - 1] = jnp.zeros((nseq,) + lg_ref.shape[2:], F32)
        vst_ref[:, ntiles - 1] = jnp.zeros((nseq,) + vst_ref.shape[2:], BF16)

    def do_page(t, p, ikt, kt, vt, is_new):
        d = jnp.dot(iq_ref[...], ikt.astype(BF16), preferred_element_type=F32)
        r = jnp.maximum(d, 0.0) * iw_ref[...]
        per_q = [jnp.sum(r[q * IDX_HEADS:(q + 1) * IDX_HEADS], axis=0, keepdims=True) for q in range(s_new)]
        sc = jnp.concatenate(per_q * reps, axis=0)
        if is_new:
            qi = lax.broadcasted_iota(I32, sc.shape, 0) & (s_new - 1)
            ki = lax.broadcasted_iota(I32, sc.shape, 1)
            sc = jnp.where(ki <= qi, sc, -jnp.inf)
        s_ref[t, p, pl.ds(row0, rows), :] = sc
        lg_ref[g, t, :, p * page:(p + 1) * page] = jnp.dot(qbd_ref[...], kt.astype(BF16),
                                                           preferred_element_type=F32)
        vst_ref[g, t, :, p * page:(p + 1) * page] = vt.astype(BF16)

    for p in range(np_):
        do_page(jj, p, cik[p][...], ck[p][...], cv[p][...], False)

    @pl.when(jj == nsteps - 1)
    def _():
        do_page(ntiles - 1, 0, nik_ref[...], nk_ref[...], nv_ref[...], True)

    @pl.when(jnp.logical_and(g == ngrp - 1, jj == nsteps - 1))
    def _():
        _select_topk(s_ref, thr_ref, ntiles, topk, idx_bits, rows_all)
        nvreg = lg_ref.shape[2] // rows
        for q in range(nseq):
            thr_full = _tile_lanes(thr_ref[q * rows:(q + 1) * rows, :], width)

            def masked(t):
                sc = jnp.concatenate([s_ref[t, j, q * rows:(q + 1) * rows, :] for j in range(np_)], axis=1)
                bias = jnp.where(sc >= thr_full, 0.0, NEG)
                return lg_ref[q, t] + jnp.concatenate([bias] * nvreg, axis=0)

            m = jnp.max(masked(0), axis=1, keepdims=True)
            for t in range(1, ntiles):
                m = jnp.maximum(m, jnp.max(masked(t), axis=1, keepdims=True))
            l = jnp.zeros((lg_ref.shape[2], 1), F32)
            acc = jnp.zeros(o_ref.shape[1:], F32)
            for t in range(ntiles):
                p = jnp.exp2(masked(t) - m)
                l = l + jnp.sum(p, axis=1, keepdims=True)
                acc = acc + lax.dot_general(p.astype(BF16), vst_ref[q, t], (((1,), (1,)), ((), ())),
                                            preferred_element_type=F32)
            o_ref[q] = acc / l


def _sattn(page_table, cache_ikt, cache_kt, cache_vt, iq_s, iw_rep, qbd, nikt, nkt, nvt, topk):
    db, pages = page_table.shape
    page = cache_ikt.shape[2]
    assert page == LANES and pages % PAGES_PER_STEP == 0
    s_new = iq_s.shape[1] // IDX_HEADS
    rows = SUBLANES
    assert rows % s_new == 0 and s_new & (s_new - 1) == 0
    nseq = math.gcd(db, SEQS_PER_GROUP)
    nrow = qbd.shape[1]
    dkv = cache_kt.shape[1]
    nsteps = pages // PAGES_PER_STEP
    width = PAGES_PER_STEP * page
    idx_bits = int((nsteps + 1) * width).bit_length()

    def page_spec(depth, p):
        return pl.BlockSpec(
            (None, depth, page),
            lambda bo, g, jj, pt, p=p: (pt[(bo * nseq + g) * pages + jj * PAGES_PER_STEP + p], 0, 0))

    per_seq = lambda shape: pl.BlockSpec((None,) + shape, lambda bo, g, jj, pt: (bo * nseq + g, 0, 0))
    in_specs = ([page_spec(IDX_DIM, p) for p in range(PAGES_PER_STEP)]
                + [page_spec(dkv, p) for p in range(PAGES_PER_STEP)]
                + [page_spec(dkv, p) for p in range(PAGES_PER_STEP)]
                + [per_seq(iq_s.shape[1:]), per_seq(iw_rep.shape[1:]), per_seq(qbd.shape[1:]),
                   per_seq(nikt.shape[1:]), per_seq(nkt.shape[1:]), per_seq(nvt.shape[1:])])
    grid_spec = pltpu.PrefetchScalarGridSpec(
        num_scalar_prefetch=1,
        grid=(db // nseq, nseq, nsteps),
        in_specs=in_specs,
        out_specs=pl.BlockSpec((nseq, nrow, dkv), lambda bo, g, jj, pt: (bo, 0, 0)),
        scratch_shapes=[
            pltpu.VMEM((nsteps + 1, PAGES_PER_STEP, nseq * rows, page), F32),
            pltpu.VMEM((nseq * rows, LANES), F32),
            pltpu.VMEM((nseq, nsteps + 1, nrow, width), F32),
            pltpu.VMEM((nseq, nsteps + 1, dkv, width), BF16),
        ],
    )
    args = ([cache_ikt] * PAGES_PER_STEP + [cache_kt] * PAGES_PER_STEP + [cache_vt] * PAGES_PER_STEP
            + [iq_s, iw_rep, qbd, nikt, nkt, nvt])
    return pl.pallas_call(
        functools.partial(_sattn_kernel, topk=topk, idx_bits=idx_bits, s_new=s_new),
        grid_spec=grid_spec,
        out_shape=jax.ShapeDtypeStruct((db, nrow, dkv), F32),
        compiler_params=pltpu.CompilerParams(dimension_semantics=("arbitrary", "arbitrary", "arbitrary"),
                                             vmem_limit_bytes=VMEM_LIMIT),
        name="sattn",
    )(page_table.reshape(-1), *args)


def _merge_kernel(x_ref, ap_ref, as_ref, z_ref, zh_ref, st1_ref, st2_ref, cb_ref, ga_ref, gb_ref,
                  wc_ref, wa_ref, wb_ref, wo_ref, g_ref, b_ref, wr_ref, br_ref,
                  h_ref, comb_ref, *, alpha, n_prompt_tiles, s_new):
    i = pl.program_id(0)
    is_prompt = i < n_prompt_tiles
    z = z_ref[...]
    row = lax.broadcasted_iota(I32, z.shape, 0)
    halo = jnp.where(i == 0, 0.0, zh_ref[...])
    prev1 = jnp.broadcast_to(halo[SUBLANES - 1:SUBLANES], z.shape)
    prev2 = jnp.broadcast_to(halo[SUBLANES - 2:SUBLANES - 1], z.shape)
    seq_row = row & (s_new - 1)
    first = jnp.where(is_prompt, row, seq_row)
    fill1 = jnp.where(is_prompt, prev1, st1_ref[...])
    fill2 = jnp.where(is_prompt, jnp.where(row == 0, prev2, prev1), st2_ref[...])
    z1 = jnp.where(first >= 1, pltpu.roll(z, 1, 0), fill1)
    z2 = jnp.where(first >= 2, pltpu.roll(z, 2, 0), fill2)
    wc = wc_ref[...]
    y = wc[0:1] * z2 + wc[1:2] * z1 + wc[2:3] * z
    conv_o = cb_ref[...] * y
    attn = jnp.where(is_prompt, ap_ref[...], as_ref[...])
    a = jnp.dot(attn.astype(BF16), wa_ref[...], preferred_element_type=F32)
    b = jnp.dot(conv_o.astype(BF16), wb_ref[...], preferred_element_type=F32)
    sig = lambda u: 1.0 / (1.0 + jnp.exp(-u))
    mixed = sig(ga_ref[...]) * a + sig(gb_ref[...]) * b
    mix = jnp.dot(mixed.astype(BF16), wo_ref[...], preferred_element_type=F32)
    h = _layer_norm(alpha * x_ref[...] + mix, g_ref[...], b_ref[...])
    h_ref[...] = h

    logits = jnp.dot(h, wr_ref[...], preferred_element_type=F32, precision=lax.Precision.HIGHEST) + br_ref[...]
    lane_i = lax.broadcasted_iota(I32, logits.shape, 1)
    lane = lane_i.astype(F32)
    first_where = lambda cond: jnp.min(jnp.where(cond, lane, float(LANES)), axis=1, keepdims=True)
    is_g = lane_i < N_GROUPS
    gl = jnp.where(is_g, logits, -jnp.inf)
    gmax = jnp.max(gl, axis=1, keepdims=True)
    g_sel = first_where(gl == gmax)
    g_p = 1.0 / jnp.sum(jnp.where(is_g, jnp.exp(logits - gmax), 0.0), axis=1, keepdims=True)
    grp = jnp.where((lane_i >= N_GROUPS) & (lane_i < N_GROUPS + N_EXPERTS),
                    lax.shift_right_arithmetic(lane_i - N_GROUPS, jnp.int32(int(math.log2(EXPERTS_PER_GROUP)))),
                    -1).astype(F32)
    in_grp = grp == g_sel
    e1 = jnp.where(in_grp, logits, -jnp.inf)
    max1 = jnp.max(e1, axis=1, keepdims=True)
    i1 = first_where(e1 == max1)
    e2 = jnp.where(lane == i1, -jnp.inf, e1)
    max2 = jnp.max(e2, axis=1, keepdims=True)
    i2 = first_where(e2 == max2)
    den = jnp.sum(jnp.where(in_grp, jnp.exp(logits - max1), 0.0), axis=1, keepdims=True)
    p1 = 1.0 / den
    p2 = jnp.exp(max2 - max1) / den
    tot = p1 + p2
    comb_ref[...] = jnp.where(lane == i1, p1 / tot * g_p, 0.0) + jnp.where(lane == i2, p2 / tot * g_p, 0.0)


def _merge(x, attn_p, attn_s, z, st1, st2, cb, ga, gb, wc, wa, wb, wo, g, b, wr, br, alpha, s_new):
    tp, d = x.shape
    npt = attn_p.shape[0] // TM
    assert attn_p.shape[0] % TM == 0 and attn_s.shape[0] == tp - npt * TM and s_new & (s_new - 1) == 0
    row = lambda width: pl.BlockSpec((TM, width), lambda i: (i, 0))
    prompt_row = lambda width: pl.BlockSpec((TM, width), lambda i: (jnp.minimum(i, npt - 1), 0))
    sample_row = lambda width: pl.BlockSpec((TM, width), lambda i: (jnp.maximum(i - npt, 0), 0))
    halo = pl.BlockSpec((SUBLANES, z.shape[1]), lambda i: (jnp.maximum(i * (TM // SUBLANES) - 1, 0), 0))
    const = lambda a: pl.BlockSpec(a.shape, lambda i: (0, 0), pipeline_mode=pl.Buffered(1))
    dc = z.shape[1]
    return pl.pallas_call(
        functools.partial(_merge_kernel, alpha=alpha, n_prompt_tiles=npt, s_new=s_new),
        grid=(tp // TM,),
        in_specs=[row(d), prompt_row(attn_p.shape[1]), sample_row(attn_s.shape[1]), row(dc), halo,
                  sample_row(dc), sample_row(dc), row(dc), row(d), row(d),
                  const(wc), const(wa), const(wb), const(wo), const(g), const(b), const(wr), const(br)],
        out_specs=[row(d), row(LANES)],
        out_shape=[jax.ShapeDtypeStruct((tp, d), F32), jax.ShapeDtypeStruct((tp, LANES), F32)],
        compiler_params=pltpu.CompilerParams(dimension_semantics=("arbitrary",), vmem_limit_bytes=VMEM_LIMIT),
        name="merge",
    )(x, attn_p, attn_s, z, z, st1, st2, cb, ga, gb, wc, wa, wb, wo, g, b, wr, br)


def _moe_kernel(h_ref, comb_ref, wgu_ref, wd_ref, g_ref, b_ref, o_ref, acc_ref, hb_ref, *, alpha):
    e = pl.program_id(1)
    de = wd_ref.shape[0]

    @pl.when(e == 0)
    def _():
        acc_ref[...] = jnp.zeros(acc_ref.shape, F32)
        hb_ref[...] = h_ref[...].astype(BF16)

    gu = jnp.dot(hb_ref[...], wgu_ref[...], preferred_element_type=F32)
    hg = gu[:, :de]
    hu = gu[:, de:]
    lane = lax.broadcasted_iota(I32, comb_ref.shape, 1)
    c = jnp.sum(jnp.where(lane == e + N_GROUPS, comb_ref[...], 0.0), axis=1, keepdims=True)
    act = hg * (1.0 / (1.0 + jnp.exp(-hg))) * hu * c
    acc_ref[...] += jnp.dot(act.astype(BF16), wd_ref[...], preferred_element_type=F32)

    @pl.when(e == pl.num_programs(1) - 1)
    def _():
        o_ref[...] = _layer_norm(alpha * h_ref[...] + acc_ref[...], g_ref[...], b_ref[...])


def _moe(h, comb, wgu, wd, g, b, alpha):
    tp, d = h.shape
    ne, _, de2 = wgu.shape
    de = wd.shape[1]
    return pl.pallas_call(
        functools.partial(_moe_kernel, alpha=alpha),
        grid=(tp // TM_MOE, ne),
        in_specs=[
            pl.BlockSpec((TM_MOE, d), lambda i, e: (i, 0)),
            pl.BlockSpec((TM_MOE, LANES), lambda i, e: (i, 0)),
            pl.BlockSpec((None, d, de2), lambda i, e: (e, 0, 0)),
            pl.BlockSpec((None, de, d), lambda i, e: (e, 0, 0)),
            pl.BlockSpec((1, d), lambda i, e: (0, 0)),
            pl.BlockSpec((1, d), lambda i, e: (0, 0)),
        ],
        out_specs=pl.BlockSpec((TM_MOE, d), lambda i, e: (i, 0)),
        out_shape=jax.ShapeDtypeStruct((tp, d), F32),
        scratch_shapes=[pltpu.VMEM((TM_MOE, d), F32), pltpu.VMEM((TM_MOE, d), BF16)],
        compiler_params=pltpu.CompilerParams(dimension_semantics=("arbitrary", "arbitrary"),
                                             vmem_limit_bytes=VMEM_LIMIT),
        name="moe",
    )(h, comb, wgu, wd, g, b)


def _rope_tables(pos):
    half = HEAD_DIM // 2
    inv = jnp.power(jnp.float32(ROPE_THETA), -jnp.arange(half, dtype=F32) * 2.0 / HEAD_DIM)
    ang = pos.astype(F32)[:, None] * inv[None, :]
    cos, sin = jnp.cos(ang), jnp.sin(ang)
    reps = LANES // HEAD_DIM
    return (jnp.concatenate([cos, cos] * reps, axis=1), jnp.concatenate([-sin, sin] * reps, axis=1))


def _pad_rows(a, rows):
    return jnp.pad(a, [(0, rows - a.shape[0])] + [(0, 0)] * (a.ndim - 1))


def kernel(x_prompt, x_sample, cache_k, cache_v, cache_idx_k, state_conv, page_table, meta_tokens, w_in, b_in, w_conv, w_attn_up, w_conv_out, w_o, ln1_g, ln1_b, w_group, b_group, w_expert_router, b_expert_router, w_gate, w_up, w_down, ln2_g, ln2_b):
    bsz, s_p, d = x_prompt.shape
    db, s_s, _ = x_sample.shape
    depth = w_in.shape[0]
    assert bsz == 1, "prompt group is served one sequence at a time"
    assert s_s >= CONV_WIDTH - 1
    n_phys, page = cache_k.shape[1], cache_k.shape[2]
    pages = page_table.shape[1]
    past_len = pages * page
    t_p = s_p + N_META
    t_s = db * s_s
    t_all = t_p + t_s
    topk_p = min(TOPK_MAX, t_p // 4)
    topk_s = min(TOPK_MAX, (past_len + s_s) // 4)
    alpha = (2 * depth) ** 0.25

    d_attn = N_HEADS * HEAD_DIM
    d_kv = KV_HEADS * HEAD_DIM
    d_idx = IDX_HEADS * IDX_DIM
    d_conv = w_conv.shape[2]
    segs, offs, _ = _proj_layout(d_attn, d_kv, d_idx, d_conv, d)
    widths = {name: width for name, width, _ in segs}
    ref_order = ["q", "k", "v", "iq", "iw", "ik", "cu", "cb", "cc", "ga", "gb"]
    ref_starts = np.concatenate([[0], np.cumsum([widths[nm] for nm in ref_order])])

    tq = -(-t_p // BQ) * BQ
    tk = -(-tq // LB) * LB
    tile = max(TM, TM_MOE, LB)
    assert tq % TM == 0 and TM % s_s == 0
    tp = -(-max(tq + t_s, tk) // tile) * tile
    nqb = tq // BQ
    sl = slice(tq, tq + t_s)

    pos = jnp.concatenate([jnp.arange(t_p, dtype=jnp.int32), jnp.zeros((tq - t_p,), jnp.int32),
                           jnp.tile(past_len + jnp.arange(s_s, dtype=jnp.int32), db),
                           jnp.zeros((tp - tq - t_s,), jnp.int32)])
    cos_t, sin_t = _rope_tables(pos)

    h = jnp.concatenate([meta_tokens.astype(x_prompt.dtype), x_prompt[0], jnp.zeros((tq - t_p, d), x_prompt.dtype),
                         x_sample.reshape(t_s, d), jnp.zeros((tp - tq - t_s, d), x_prompt.dtype)], axis=0)

    q_scale = HEAD_DIM ** -0.5 * math.log2(math.e)

    outs = {k: [] for k in ("kp", "vp", "ikp", "cp", "ks", "vs", "iks", "cs")}
    for l in range(depth):
        pieces_w, pieces_b = [], []
        for nm, width, padded in segs:
            a = int(ref_starts[ref_order.index(nm)])
            pieces_w.append(jnp.pad(w_in[l][:, a:a + width], [(0, 0), (0, padded - width)]))
            pieces_b.append(jnp.pad(b_in[l][a:a + width], [(0, padded - width)]))
        w_p = jnp.concatenate(pieces_w, axis=1).astype(BF16)
        b_p = jnp.concatenate(pieces_b)[None, :]

        (q_r, iq_r, k_r, v_r, ik_r, kt_b, ikt_b, v_b, iw_r, z, cb, ga, gb) = _proj(
            h, w_p, b_p, cos_t, sin_t, offs, widths, q_scale)

        attn_p = _pattn(q_r, iq_r, iw_r, ikt_b, kt_b, v_b.reshape(tp // LB, LB, d_kv), topk_p, nqb)

        iq_s = iq_r[sl].astype(BF16).reshape(db, s_s * IDX_HEADS, IDX_DIM)
        iw_rep = jnp.broadcast_to(iw_r[sl].reshape(db, s_s * IDX_HEADS, 1), (db, s_s * IDX_HEADS, LANES))
        q_s = q_r[sl].astype(BF16).reshape(db, s_s, KV_HEADS, Q_PER_KV, HEAD_DIM)
        q_s = q_s.transpose(0, 2, 3, 1, 4)
        eye = jnp.eye(KV_HEADS, dtype=BF16)
        qbd = (q_s[:, :, :, :, None, :] * eye[None, :, None, None, :, None])
        qbd = qbd.reshape(db, KV_HEADS * Q_PER_KV * s_s, d_kv)
        new_page_t = lambda a: jnp.pad(a[sl].reshape(db, s_s, a.shape[1]).transpose(0, 2, 1),
                                       [(0, 0), (0, 0), (0, page - s_s)])
        cache_ikt = cache_idx_k[l].transpose(0, 2, 1)
        cache_kt = cache_k[l].transpose(0, 2, 3, 1).reshape(n_phys, d_kv, page)
        cache_vt = cache_v[l].transpose(0, 2, 3, 1).reshape(n_phys, d_kv, page)
        r_s = _sattn(page_table, cache_ikt, cache_kt, cache_vt, iq_s, iw_rep, qbd,
                     new_page_t(ik_r), new_page_t(k_r), new_page_t(v_r), topk_s)
        r_s = r_s.reshape(db, KV_HEADS, Q_PER_KV, s_s, KV_HEADS, HEAD_DIM)
        attn_s = jnp.stack([r_s[:, kvh, :, :, kvh, :] for kvh in range(KV_HEADS)], axis=1)
        attn_s = attn_s.transpose(0, 3, 1, 2, 4).reshape(t_s, d_attn)

        state = state_conv[l].astype(F32)
        assert CONV_WIDTH == 3
        st1 = jnp.broadcast_to(state[:, 1:2], (db, s_s, d_conv)).reshape(t_s, d_conv)
        st2 = jnp.concatenate([state[:, 0:1], jnp.broadcast_to(state[:, 1:2], (db, s_s - 1, d_conv))], axis=1)
        st2 = st2.reshape(t_s, d_conv)

        wr = jnp.pad(jnp.concatenate([w_group[l], w_expert_router[l]], axis=1),
                     [(0, 0), (0, LANES - N_GROUPS - N_EXPERTS)])
        br = jnp.pad(jnp.concatenate([b_group[l], b_expert_router[l]]), [(0, LANES - N_GROUPS - N_EXPERTS)])[None]
        h1, comb = _merge(h, attn_p, _pad_rows(attn_s, tp - tq), z, _pad_rows(st1, tp - tq),
                          _pad_rows(st2, tp - tq), cb, ga, gb, w_conv[l],
                          w_attn_up[l].astype(BF16), w_conv_out[l].astype(BF16), w_o[l].astype(BF16),
                          ln1_g[l][None], ln1_b[l][None], wr, br, alpha, s_s)

        wgu = jnp.concatenate([w_gate[l], w_up[l]], axis=2).astype(BF16)
        h = _moe(h1, comb, wgu, w_down[l].astype(BF16), ln2_g[l][None], ln2_b[l][None], alpha)

        outs["kp"].append(k_r[:t_p].reshape(bsz, t_p, KV_HEADS, HEAD_DIM))
        outs["vp"].append(v_r[:t_p].reshape(bsz, t_p, KV_HEADS, HEAD_DIM))
        outs["ikp"].append(ik_r[:t_p].reshape(bsz, t_p, IDX_DIM))
        outs["cp"].append(z[t_p - (CONV_WIDTH - 1):t_p].reshape(bsz, CONV_WIDTH - 1, d_conv))
        outs["ks"].append(k_r[sl].reshape(db, s_s, KV_HEADS, HEAD_DIM))
        outs["vs"].append(v_r[sl].reshape(db, s_s, KV_HEADS, HEAD_DIM))
        outs["iks"].append(ik_r[sl].reshape(db, s_s, IDX_DIM))
        outs["cs"].append(z[sl].reshape(db, s_s, d_conv)[:, -(CONV_WIDTH - 1):])

    y_prompt = h[N_META:t_p].reshape(bsz, s_p, d)
    y_sample = h[sl].reshape(db, s_s, d)
    st = lambda k: jnp.stack(outs[k])
    return (y_prompt, y_sample, st("kp"), st("vp"), st("ikp"), st("cp"), st("ks"), st("vs"), st("iks"), st("cs"))
```

```python
import functools
import math

import numpy as np
import jax
import jax.numpy as jnp
from jax import lax
from jax.experimental import pallas as pl
from jax.experimental.pallas import tpu as pltpu

F32 = jnp.float32
BF16 = jnp.bfloat16
I32 = jnp.int32

N_META = 16
N_HEADS = 8
HEAD_DIM = 64
KV_HEADS = 4
Q_PER_KV = N_HEADS // KV_HEADS
IDX_HEADS = 8
IDX_DIM = 64
TOPK_MAX = 256
CONV_WIDTH = 3
N_GROUPS = 4
EXPERTS_PER_GROUP = 4
N_EXPERTS = N_GROUPS * EXPERTS_PER_GROUP
ROPE_THETA = 10000.0
LN_EPS = 1e-5
IDX_W_SCALE = (IDX_HEADS ** -0.5) * (IDX_DIM ** -0.5)

LANES = 128
SUBLANES = 8
INT_MIN = -(2 ** 31)
KEY_NEG_FLT_MAX = INT_MIN + (1 << 23)
FLT_MAX = float(np.finfo(np.float32).max)
NEG = -1e30
VMEM_LIMIT = 56 * 1024 * 1024

TM = 256
TM_MOE = 1024
BQ = 256
LB = 512
ROW_GROUP = 128
PAGES_PER_STEP = 16
SEQS_PER_GROUP = 4


def _tile_lanes(x, n):
    reps = n // x.shape[1]
    return x if reps == 1 else jnp.concatenate([x] * reps, axis=1)


def _layer_norm(x, g, b):
    mu = jnp.mean(x, axis=-1, keepdims=True)
    xc = x - mu
    var = jnp.mean(xc * xc, axis=-1, keepdims=True)
    return xc * lax.rsqrt(var + LN_EPS) * g + b


def _proj_layout(d_attn, d_kv, d_idx, d_conv, d_model):
    segs = [("q", d_attn, d_attn), ("k", d_kv, d_kv), ("v", d_kv, d_kv), ("iq", d_idx, d_idx),
            ("iw", IDX_HEADS, LANES), ("ik", IDX_DIM, LANES),
            ("cu", d_conv, d_conv), ("cb", d_conv, d_conv), ("cc", d_conv, d_conv),
            ("ga", d_model, d_model), ("gb", d_model, d_model)]
    offs, o = {}, 0
    for name, _, padded in segs:
        offs[name] = (o, padded)
        o += padded
    return segs, offs, o


def _proj_kernel(x_ref, w_ref, b_ref, cos_ref, sin_ref,
                 q_ref, iq_ref, k_ref, v_ref, ik_ref, kt_ref, ikt_ref, vb_ref, iw_ref, z_ref, cb_ref, ga_ref, gb_ref,
                 *, offs, q_scale):
    xb = x_ref[...].astype(BF16)

    def seg(name):
        a, n = offs[name]
        return jnp.dot(xb, w_ref[:, a:a + n], preferred_element_type=F32) + b_ref[:, a:a + n]

    cos = cos_ref[...]
    sin = sin_ref[...]

    def rope(y):
        n = y.shape[1]
        lane = lax.broadcasted_iota(I32, y.shape, 1)
        first_half = (lane & (HEAD_DIM - 1)) < (HEAD_DIM // 2)
        swapped = jnp.where(first_half, pltpu.roll(y, n - HEAD_DIM // 2, 1), pltpu.roll(y, HEAD_DIM // 2, 1))
        return y * _tile_lanes(cos, n) + swapped * _tile_lanes(sin, n)

    q_ref[...] = rope(seg("q")) * q_scale
    iq_ref[...] = rope(seg("iq"))
    k = rope(seg("k"))
    k_ref[...] = k
    kt_ref[...] = k.T.astype(BF16)
    v = seg("v")
    v_ref[...] = v
    vb_ref[...] = v.astype(BF16)
    ik = rope(seg("ik"))
    ik_ref[...] = ik[:, :IDX_DIM]
    ikt_ref[...] = ik.T[:IDX_DIM].astype(BF16)
    iw_ref[...] = seg("iw")[:, :IDX_HEADS] * IDX_W_SCALE
    z_ref[...] = seg("cc") * seg("cu")
    cb_ref[...] = seg("cb")
    ga_ref[...] = seg("ga")
    gb_ref[...] = seg("gb")


def _proj(x, w, b, cos, sin, offs, widths, q_scale):
    tp, d = x.shape
    n = w.shape[1]
    tm = LB
    nb = tp // tm
    row = lambda width: pl.BlockSpec((tm, width), lambda i: (i, 0))
    blk = lambda depth: pl.BlockSpec((None, depth, tm), lambda i: (i, 0, 0))
    const = lambda shape: pl.BlockSpec(shape, lambda i: (0, 0), pipeline_mode=pl.Buffered(1))
    dq, dkv, dc, dm = widths["q"], widths["k"], widths["cu"], widths["ga"]
    f32 = lambda width: jax.ShapeDtypeStruct((tp, width), F32)
    out = [(row(dq), f32(dq)), (row(widths["iq"]), f32(widths["iq"])),
           (row(dkv), f32(dkv)), (row(dkv), f32(dkv)), (row(IDX_DIM), f32(IDX_DIM)),
           (blk(dkv), jax.ShapeDtypeStruct((nb, dkv, tm), BF16)),
           (blk(IDX_DIM), jax.ShapeDtypeStruct((nb, IDX_DIM, tm), BF16)),
           (row(dkv), jax.ShapeDtypeStruct((tp, dkv), BF16)),
           (row(IDX_HEADS), f32(IDX_HEADS)), (row(dc), f32(dc)), (row(dc), f32(dc)),
           (row(dm), f32(dm)), (row(dm), f32(dm))]
    return pl.pallas_call(
        functools.partial(_proj_kernel, offs=offs, q_scale=q_scale),
        grid=(nb,),
        in_specs=[row(d), const((d, n)), const((1, n)), row(LANES), row(LANES)],
        out_specs=[o[0] for o in out],
        out_shape=[o[1] for o in out],
        compiler_params=pltpu.CompilerParams(dimension_semantics=("arbitrary",), vmem_limit_bytes=VMEM_LIMIT),
        name="proj",
    )(x, w, b, cos, sin)


def _key_to_float(c):
    bits = jnp.where(c >= 0, c, c ^ jnp.int32(0x7FFFFFFF))
    return lax.bitcast_convert_type(bits, F32)


def _float_to_key(x):
    bits = lax.bitcast_convert_type(x, I32)
    return jnp.where(bits >= 0, bits, bits ^ jnp.int32(0x7FFFFFFF))


def _select_topk(s_ref, thr_ref, ntiles, topk, idx_bits, rg):
    _, nslab, rows, _ = s_ref.shape
    width = nslab * LANES
    nbin = -(-topk // LANES)
    assert nbin <= nslab and rows % rg == 0
    static = isinstance(ntiles, int)

    def loop(body, init):
        if not static:
            return lax.fori_loop(0, ntiles, body, init)
        carry = init
        for t in range(ntiles):
            carry = body(t, carry)
        return carry

    for r0 in range(0, rows, rg):
        lane = lax.broadcasted_iota(I32, (rg, LANES), 1)
        slab = lambda t, j: s_ref[t, j, r0:r0 + rg, :]

        def count(pred):
            def body(t, cnt):
                for j in range(nslab):
                    cnt = cnt + jnp.where(pred(slab(t, j), t * width + j * LANES + lane), 1.0, 0.0)
                return cnt
            return jnp.sum(loop(body, jnp.zeros((rg, LANES), F32)), axis=1, keepdims=True)

        def bin_body(t, bm):
            bm = list(bm)
            for j in range(nslab):
                bm[j % nbin] = jnp.maximum(bm[j % nbin], slab(t, j))
            return tuple(bm)

        bm = loop(bin_body, tuple(jnp.full((rg, LANES), -jnp.inf, F32) for _ in range(nbin)))
        lo, hi = bm[0], bm[0]
        for j in range(1, nbin):
            lo, hi = jnp.minimum(lo, bm[j]), jnp.maximum(hi, bm[j])
        lo_f = jnp.broadcast_to(jnp.min(lo, axis=1, keepdims=True), (rg, LANES))
        hi_f = jnp.broadcast_to(jnp.max(hi, axis=1, keepdims=True), (rg, LANES))

        def unfinished(lo_k, c_lo, hi_k):
            open_ = jnp.where(c_lo == topk, 0.0, jnp.where(hi_k - 1 > lo_k, 1.0, 0.0))
            return (jnp.max(open_) > 0.5).astype(I32)

        def cond(st):
            return st[0] > 0

        def body(st):
            _, lo_k, c_lo, hi_k, c_hi = st
            mid = (lo_k >> 1) + (hi_k >> 1) + (lo_k & hi_k & 1)
            mid_f = _key_to_float(mid)
            cnt = count(lambda s, idx: s >= mid_f)
            ge = cnt >= topk
            lo_k, c_lo = jnp.where(ge, mid, lo_k), jnp.where(ge, cnt, c_lo)
            hi_k, c_hi = jnp.where(ge, hi_k, mid), jnp.where(ge, c_hi, cnt)
            return unfinished(lo_k, c_lo, hi_k), lo_k, c_lo, hi_k, c_hi

        lo_k0 = _float_to_key(lo_f)
        hi_k0 = _float_to_key(hi_f) + 1
        c_lo0 = jnp.broadcast_to(count(lambda s, idx: s >= lo_f), (rg, LANES))
        _, lo_k, c_lo, _, c_hi = lax.while_loop(
            cond, body, (unfinished(lo_k0, c_lo0, hi_k0), lo_k0, c_lo0, hi_k0, jnp.zeros((rg, LANES), F32)))
        finite = lo_k >= KEY_NEG_FLT_MAX
        thr = jnp.where(finite, _key_to_float(lo_k), -FLT_MAX)
        thr_ref[r0:r0 + rg, :] = thr

        tied = jnp.where(finite, jnp.where(c_lo > topk, 1.0, 0.0), 0.0)

        @pl.when(jnp.max(tied) > 0.5)
        def _():
            need = jnp.where(tied > 0.5, topk - c_hi, float(2 ** 30))

            def index_bit(bi, x):
                cand = x + lax.shift_left(jnp.int32(1), idx_bits - 1 - bi)
                cnt = count(lambda s, idx: jnp.where(s == thr, idx, cand) < cand)
                return jnp.where(cnt < need, cand, x)

            x = lax.fori_loop(0, idx_bits, index_bit, jnp.zeros((rg, LANES), I32))

            def fix(t_, c):
                for j in range(nslab):
                    sj = slab(t_, j)
                    drop = jnp.where(sj == thr, t_ * width + j * LANES + lane, x) > x
                    s_ref[t_, j, r0:r0 + rg, :] = jnp.where(drop, -jnp.inf, sj)
                return c

            loop(fix, 0)


def _pattn_kernel(q_ref, iq_ref, iw_ref, ikt_ref, kt_ref, v_ref, o_ref,
                  s_ref, thr_ref, wb_ref, qg_ref, iqh_ref, m_ref, l_ref, acc_ref, *, topk, idx_bits):
    bq = o_ref.shape[0]
    lb = s_ref.shape[1] * LANES
    i = pl.program_id(0)
    nkb = pl.cdiv((i + 1) * bq, lb)

    for h in range(IDX_HEADS):
        wb_ref[h] = jnp.broadcast_to(iw_ref[:, h:h + 1], (bq, LANES))
        iqh_ref[h * bq:(h + 1) * bq, :] = iq_ref[:, h * IDX_DIM:(h + 1) * IDX_DIM].astype(BF16)
    for h in range(N_HEADS):
        kvh, g = divmod(h, Q_PER_KV)
        qg_ref[kvh, g * bq:(g + 1) * bq, :] = q_ref[:, h * HEAD_DIM:(h + 1) * HEAD_DIM].astype(BF16)

    row = lax.broadcasted_iota(I32, (bq, lb), 0)
    col = lax.broadcasted_iota(I32, (bq, lb), 1)

    def scores(kb, c):
        d = jnp.dot(iqh_ref[...], ikt_ref[kb], preferred_element_type=F32)
        acc = jnp.zeros((bq, lb), F32)
        for h in range(IDX_HEADS):
            acc = acc + _tile_lanes(wb_ref[h], lb) * jnp.maximum(d[h * bq:(h + 1) * bq], 0.0)
        sc = jnp.where(kb * lb + col <= i * bq + row, acc, -jnp.inf)
        for j in range(lb // LANES):
            s_ref[kb, j] = sc[:, j * LANES:(j + 1) * LANES]
        return c

    lax.fori_loop(0, nkb, scores, 0)

    _select_topk(s_ref, thr_ref, nkb, topk, idx_bits, ROW_GROUP)
    thr_full = _tile_lanes(thr_ref[...], lb)

    m_ref[...] = jnp.full(m_ref.shape, NEG, F32)
    l_ref[...] = jnp.zeros(l_ref.shape, F32)
    acc_ref[...] = jnp.zeros(acc_ref.shape, F32)

    def attend(kb, c):
        sc = jnp.concatenate([s_ref[kb, j] for j in range(lb // LANES)], axis=1)
        bias = jnp.where(sc >= thr_full, 0.0, NEG)
        bias2 = jnp.concatenate([bias] * Q_PER_KV, axis=0)
        for kvh in range(KV_HEADS):
            s = jnp.dot(qg_ref[kvh], kt_ref[kb, kvh * HEAD_DIM:(kvh + 1) * HEAD_DIM, :],
                        preferred_element_type=F32) + bias2
            m_prev = m_ref[kvh]
            m_new = jnp.maximum(m_prev, jnp.max(s, axis=1, keepdims=True))
            alpha = jnp.exp2(m_prev - m_new)
            p = jnp.exp2(s - _tile_lanes(m_new, lb))
            psum = p[:, :LANES]
            for j in range(1, lb // LANES):
                psum = psum + p[:, j * LANES:(j + 1) * LANES]
            l_ref[kvh] = alpha * l_ref[kvh] + psum
            pair = kvh // 2
            pv = jnp.dot(p.astype(BF16), v_ref[kb, :, pair * LANES:(pair + 1) * LANES],
                         preferred_element_type=F32)
            acc_ref[kvh] = alpha * acc_ref[kvh] + pv
            m_ref[kvh] = m_new
        return c

    lax.fori_loop(0, nkb, attend, 0)

    for kvh in range(KV_HEADS):
        denom = jnp.sum(l_ref[kvh], axis=1, keepdims=True)
        off = (kvh % 2) * HEAD_DIM
        o = acc_ref[kvh][:, off:off + HEAD_DIM] / denom
        for g in range(Q_PER_KV):
            h = kvh * Q_PER_KV + g
            o_ref[:, h * HEAD_DIM:(h + 1) * HEAD_DIM] = o[g * bq:(g + 1) * bq]


def _pattn(q, iq, iw, ikt, kt, vv, topk, nqb):
    bq = BQ
    rows2 = Q_PER_KV * bq
    lb = ikt.shape[2]
    tq = nqb * bq
    nkb = pl.cdiv(tq, lb)
    assert nkb <= ikt.shape[0]
    idx_bits = int(nkb * lb).bit_length()
    resident = lambda shape: pl.BlockSpec(shape, lambda i: (0,) * len(shape), pipeline_mode=pl.Buffered(1))
    return pl.pallas_call(
        functools.partial(_pattn_kernel, topk=topk, idx_bits=idx_bits),
        grid=(nqb,),
        in_specs=[
            pl.BlockSpec((bq, q.shape[1]), lambda i: (i, 0)),
            pl.BlockSpec((bq, iq.shape[1]), lambda i: (i, 0)),
            pl.BlockSpec((bq, IDX_HEADS), lambda i: (i, 0)),
            resident(ikt.shape), resident(kt.shape), resident(vv.shape),
        ],
        out_specs=pl.BlockSpec((bq, N_HEADS * HEAD_DIM), lambda i: (i, 0)),
        out_shape=jax.ShapeDtypeStruct((tq, N_HEADS * HEAD_DIM), F32),
        scratch_shapes=[
            pltpu.VMEM((nkb, lb // LANES, bq, LANES), F32),
            pltpu.VMEM((bq, LANES), F32),
            pltpu.VMEM((IDX_HEADS, bq, LANES), F32),
            pltpu.VMEM((KV_HEADS, rows2, HEAD_DIM), BF16),
            pltpu.VMEM((IDX_HEADS * bq, IDX_DIM), BF16),
            pltpu.VMEM((KV_HEADS, rows2, LANES), F32),
            pltpu.VMEM((KV_HEADS, rows2, LANES), F32),
            pltpu.VMEM((KV_HEADS, rows2, LANES), F32),
        ],
        compiler_params=pltpu.CompilerParams(dimension_semantics=("arbitrary",), vmem_limit_bytes=VMEM_LIMIT),
        name="pattn",
    )(q, iq, iw, ikt, kt, vv)


def _sattn_kernel(pt_ref, *refs, topk, idx_bits, s_new):
    del pt_ref
    np_ = PAGES_PER_STEP
    cik = refs[0:np_]
    ck = refs[np_:2 * np_]
    cv = refs[2 * np_:3 * np_]
    iq_ref, iw_ref, qbd_ref, nik_ref, nk_ref, nv_ref, o_ref, s_ref, thr_ref, lg_ref, vst_ref = refs[3 * np_:]
    g = pl.program_id(1)
    jj = pl.program_id(2)
    ngrp, nsteps = pl.num_programs(1), pl.num_programs(2)
    nseq = o_ref.shape[0]
    ntiles, _, rows_all, page = s_ref.shape
    rows = rows_all // nseq
    width = np_ * page
    reps = rows // s_new
    row0 = pl.multiple_of(g * rows, rows)

    @pl.when(jnp.logical_and(pl.program_id(0) == 0, jnp.logical_and(g == 0, jj == 0)))
    def _():
        s_ref[ntiles - 1] = jnp.full(s_ref.shape[1:], -jnp.inf, F32)
        lg_ref[:, ntiles - 1] = jnp.zeros((nseq,) + lg_ref.shape[2:], F32)
        vst_ref[:, ntiles - 1] = jnp.zeros((nseq,) + vst_ref.shape[2:], BF16)

    def do_page(t, p, ikt, kt, vt, is_new):
        d = jnp.dot(iq_ref[...], ikt.astype(BF16), preferred_element_type=F32)
        r = jnp.maximum(d, 0.0) * iw_ref[...]
        per_q = [jnp.sum(r[q * IDX_HEADS:(q + 1) * IDX_HEADS], axis=0, keepdims=True) for q in range(s_new)]
        sc = jnp.concatenate(per_q * reps, axis=0)
        if is_new:
            qi = lax.broadcasted_iota(I32, sc.shape, 0) & (s_new - 1)
            ki = lax.broadcasted_iota(I32, sc.shape, 1)
            sc = jnp.where(ki <= qi, sc, -jnp.inf)
        s_ref[t, p, pl.ds(row0, rows), :] = sc
        lg_ref[g, t, :, p * page:(p + 1) * page] = jnp.dot(qbd_ref[...], kt.astype(BF16),
                                                           preferred_element_type=F32)
        vst_ref[g, t, :, p * page:(p + 1) * page] = vt.astype(BF16)

    for p in range(np_):
        do_page(jj, p, cik[p][...], ck[p][...], cv[p][...], False)

    @pl.when(jj == nsteps - 1)
    def _():
        do_page(ntiles - 1, 0, nik_ref[...], nk_ref[...], nv_ref[...], True)

    @pl.when(jnp.logical_and(g == ngrp - 1, jj == nsteps - 1))
    def _():
        _select_topk(s_ref, thr_ref, ntiles, topk, idx_bits, rows_all)
        nvreg = lg_ref.shape[2] // rows
        for q in range(nseq):
            thr_full = _tile_lanes(thr_ref[q * rows:(q + 1) * rows, :], width)

            def masked(t):
                sc = jnp.concatenate([s_ref[t, j, q * rows:(q + 1) * rows, :] for j in range(np_)], axis=1)
                bias = jnp.where(sc >= thr_full, 0.0, NEG)
                return lg_ref[q, t] + jnp.concatenate([bias] * nvreg, axis=0)

            m = jnp.max(masked(0), axis=1, keepdims=True)
            for t in range(1, ntiles):
                m = jnp.maximum(m, jnp.max(masked(t), axis=1, keepdims=True))
            l = jnp.zeros((lg_ref.shape[2], 1), F32)
            acc = jnp.zeros(o_ref.shape[1:], F32)
            for t in range(ntiles):
                p = jnp.exp2(masked(t) - m)
                l = l + jnp.sum(p, axis=1, keepdims=True)
                acc = acc + lax.dot_general(p.astype(BF16), vst_ref[q, t], (((1,), (1,)), ((), ())),
                                            preferred_element_type=F32)
            o_ref[q] = acc / l


def _sattn(page_table, cache_ikt, cache_kt, cache_vt, iq_s, iw_rep, qbd, nikt, nkt, nvt, topk):
    db, pages = page_table.shape
    page = cache_ikt.shape[2]
    assert page == LANES and pages % PAGES_PER_STEP == 0
    s_new = iq_s.shape[1] // IDX_HEADS
    rows = SUBLANES
    assert rows % s_new == 0 and s_new & (s_new - 1) == 0
    nseq = math.gcd(db, SEQS_PER_GROUP)
    nrow = qbd.shape[1]
    dkv = cache_kt.shape[1]
    nsteps = pages // PAGES_PER_STEP
    width = PAGES_PER_STEP * page
    idx_bits = int((nsteps + 1) * width).bit_length()

    def page_spec(depth, p):
        return pl.BlockSpec(
            (None, depth, page),
            lambda bo, g, jj, pt, p=p: (pt[(bo * nseq + g) * pages + jj * PAGES_PER_STEP + p], 0, 0))

    per_seq = lambda shape: pl.BlockSpec((None,) + shape, lambda bo, g, jj, pt: (bo * nseq + g, 0, 0))
    in_specs = ([page_spec(IDX_DIM, p) for p in range(PAGES_PER_STEP)]
                + [page_spec(dkv, p) for p in range(PAGES_PER_STEP)]
                + [page_spec(dkv, p) for p in range(PAGES_PER_STEP)]
                + [per_seq(iq_s.shape[1:]), per_seq(iw_rep.shape[1:]), per_seq(qbd.shape[1:]),
                   per_seq(nikt.shape[1:]), per_seq(nkt.shape[1:]), per_seq(nvt.shape[1:])])
    grid_spec = pltpu.PrefetchScalarGridSpec(
        num_scalar_prefetch=1,
        grid=(db // nseq, nseq, nsteps),
        in_specs=in_specs,
        out_specs=pl.BlockSpec((nseq, nrow, dkv), lambda bo, g, jj, pt: (bo, 0, 0)),
        scratch_shapes=[
            pltpu.VMEM((nsteps + 1, PAGES_PER_STEP, nseq * rows, page), F32),
            pltpu.VMEM((nseq * rows, LANES), F32),
            pltpu.VMEM((nseq, nsteps + 1, nrow, width), F32),
            pltpu.VMEM((nseq, nsteps + 1, dkv, width), BF16),
        ],
    )
    args = ([cache_ikt] * PAGES_PER_STEP + [cache_kt] * PAGES_PER_STEP + [cache_vt] * PAGES_PER_STEP
            + [iq_s, iw_rep, qbd, nikt, nkt, nvt])
    return pl.pallas_call(
        functools.partial(_sattn_kernel, topk=topk, idx_bits=idx_bits, s_new=s_new),
        grid_spec=grid_spec,
        out_shape=jax.ShapeDtypeStruct((db, nrow, dkv), F32),
        compiler_params=pltpu.CompilerParams(dimension_semantics=("arbitrary", "arbitrary", "arbitrary"),
                                             vmem_limit_bytes=VMEM_LIMIT),
        name="sattn",
    )(page_table.reshape(-1), *args)


def _merge_kernel(x_ref, ap_ref, as_ref, z_ref, zh_ref, st1_ref, st2_ref, cb_ref, ga_ref, gb_ref,
                  wc_ref, wa_ref, wb_ref, wo_ref, g_ref, b_ref, wr_ref, br_ref,
                  h_ref, comb_ref, *, alpha, n_prompt_tiles, s_new):
    i = pl.program_id(0)
    is_prompt = i < n_prompt_tiles
    z = z_ref[...]
    row = lax.broadcasted_iota(I32, z.shape, 0)
    halo = jnp.where(i == 0, 0.0, zh_ref[...])
    prev1 = jnp.broadcast_to(halo[SUBLANES - 1:SUBLANES], z.shape)
    prev2 = jnp.broadcast_to(halo[SUBLANES - 2:SUBLANES - 1], z.shape)
    seq_row = row & (s_new - 1)
    first = jnp.where(is_prompt, row, seq_row)
    fill1 = jnp.where(is_prompt, prev1, st1_ref[...])
    fill2 = jnp.where(is_prompt, jnp.where(row == 0, prev2, prev1), st2_ref[...])
    z1 = jnp.where(first >= 1, pltpu.roll(z, 1, 0), fill1)
    z2 = jnp.where(first >= 2, pltpu.roll(z, 2, 0), fill2)
    wc = wc_ref[...]
    y = wc[0:1] * z2 + wc[1:2] * z1 + wc[2:3] * z
    conv_o = cb_ref[...] * y
    attn = jnp.where(is_prompt, ap_ref[...], as_ref[...])
    a = jnp.dot(attn.astype(BF16), wa_ref[...], preferred_element_type=F32)
    b = jnp.dot(conv_o.astype(BF16), wb_ref[...], preferred_element_type=F32)
    sig = lambda u: 1.0 / (1.0 + jnp.exp(-u))
    mixed = sig(ga_ref[...]) * a + sig(gb_ref[...]) * b
    mix = jnp.dot(mixed.astype(BF16), wo_ref[...], preferred_element_type=F32)
    h = _layer_norm(alpha * x_ref[...] + mix, g_ref[...], b_ref[...])
    h_ref[...] = h

    logits = jnp.dot(h, wr_ref[...], preferred_element_type=F32, precision=lax.Precision.HIGHEST) + br_ref[...]
    lane_i = lax.broadcasted_iota(I32, logits.shape, 1)
    lane = lane_i.astype(F32)
    first_where = lambda cond: jnp.min(jnp.where(cond, lane, float(LANES)), axis=1, keepdims=True)
    is_g = lane_i < N_GROUPS
    gl = jnp.where(is_g, logits, -jnp.inf)
    gmax = jnp.max(gl, axis=1, keepdims=True)
    g_sel = first_where(gl == gmax)
    g_p = 1.0 / jnp.sum(jnp.where(is_g, jnp.exp(logits - gmax), 0.0), axis=1, keepdims=True)
    grp = jnp.where((lane_i >= N_GROUPS) & (lane_i < N_GROUPS + N_EXPERTS),
                    lax.shift_right_arithmetic(lane_i - N_GROUPS, jnp.int32(int(math.log2(EXPERTS_PER_GROUP)))),
                    -1).astype(F32)
    in_grp = grp == g_sel
    e1 = jnp.where(in_grp, logits, -jnp.inf)
    max1 = jnp.max(e1, axis=1, keepdims=True)
    i1 = first_where(e1 == max1)
    e2 = jnp.where(lane == i1, -jnp.inf, e1)
    max2 = jnp.max(e2, axis=1, keepdims=True)
    i2 = first_where(e2 == max2)
    den = jnp.sum(jnp.where(in_grp, jnp.exp(logits - max1), 0.0), axis=1, keepdims=True)
    p1 = 1.0 / den
    p2 = jnp.exp(max2 - max1) / den
    tot = p1 + p2
    comb_ref[...] = jnp.where(lane == i1, p1 / tot * g_p, 0.0) + jnp.where(lane == i2, p2 / tot * g_p, 0.0)


def _merge(x, attn_p, attn_s, z, st1, st2, cb, ga, gb, wc, wa, wb, wo, g, b, wr, br, alpha, s_new):
    tp, d = x.shape
    npt = attn_p.shape[0] // TM
    assert attn_p.shape[0] % TM == 0 and attn_s.shape[0] == tp - npt * TM and s_new & (s_new - 1) == 0
    row = lambda width: pl.BlockSpec((TM, width), lambda i: (i, 0))
    prompt_row = lambda width: pl.BlockSpec((TM, width), lambda i: (jnp.minimum(i, npt - 1), 0))
    sample_row = lambda width: pl.BlockSpec((TM, width), lambda i: (jnp.maximum(i - npt, 0), 0))
    halo = pl.BlockSpec((SUBLANES, z.shape[1]), lambda i: (jnp.maximum(i * (TM // SUBLANES) - 1, 0), 0))
    const = lambda a: pl.BlockSpec(a.shape, lambda i: (0, 0), pipeline_mode=pl.Buffered(1))
    dc = z.shape[1]
    return pl.pallas_call(
        functools.partial(_merge_kernel, alpha=alpha, n_prompt_tiles=npt, s_new=s_new),
        grid=(tp // TM,),
        in_specs=[row(d), prompt_row(attn_p.shape[1]), sample_row(attn_s.shape[1]), row(dc), halo,
                  sample_row(dc), sample_row(dc), row(dc), row(d), row(d),
                  const(wc), const(wa), const(wb), const(wo), const(g), const(b), const(wr), const(br)],
        out_specs=[row(d), row(LANES)],
        out_shape=[jax.ShapeDtypeStruct((tp, d), F32), jax.ShapeDtypeStruct((tp, LANES), F32)],
        compiler_params=pltpu.CompilerParams(dimension_semantics=("arbitrary",), vmem_limit_bytes=VMEM_LIMIT),
        name="merge",
    )(x, attn_p, attn_s, z, z, st1, st2, cb, ga, gb, wc, wa, wb, wo, g, b, wr, br)


def _moe_kernel(h_ref, comb_ref, wgu_ref, wd_ref, g_ref, b_ref, o_ref, acc_ref, hb_ref, *, alpha):
    step = pl.program_id(1)
    per_step, de = wd_ref.shape[0], wd_ref.shape[1]

    @pl.when(step == 0)
    def _():
        acc_ref[...] = jnp.zeros(acc_ref.shape, F32)
        hb_ref[...] = h_ref[...].astype(BF16)

    lane = lax.broadcasted_iota(I32, comb_ref.shape, 1)
    out = acc_ref[...]
    for k in range(per_step):
        e = step * per_step + k
        gu = jnp.dot(hb_ref[...], wgu_ref[k], preferred_element_type=F32)
        hg = gu[:, :de]
        hu = gu[:, de:]
        c = jnp.sum(jnp.where(lane == e + N_GROUPS, comb_ref[...], 0.0), axis=1, keepdims=True)
        act = hg * (1.0 / (1.0 + jnp.exp(-hg))) * hu * c
        out = out + jnp.dot(act.astype(BF16), wd_ref[k], preferred_element_type=F32)
    acc_ref[...] = out

    @pl.when(step == pl.num_programs(1) - 1)
    def _():
        o_ref[...] = _layer_norm(alpha * h_ref[...] + acc_ref[...], g_ref[...], b_ref[...])


def _moe(h, comb, wgu, wd, g, b, alpha):
    tp, d = h.shape
    ne, _, de2 = wgu.shape
    de = wd.shape[1]
    per_step = EXPERTS_PER_GROUP
    assert ne % per_step == 0
    return pl.pallas_call(
        functools.partial(_moe_kernel, alpha=alpha),
        grid=(tp // TM_MOE, ne // per_step),
        in_specs=[
            pl.BlockSpec((TM_MOE, d), lambda i, e: (i, 0)),
            pl.BlockSpec((TM_MOE, LANES), lambda i, e: (i, 0)),
            pl.BlockSpec((per_step, d, de2), lambda i, e: (e, 0, 0)),
            pl.BlockSpec((per_step, de, d), lambda i, e: (e, 0, 0)),
            pl.BlockSpec((1, d), lambda i, e: (0, 0)),
            pl.BlockSpec((1, d), lambda i, e: (0, 0)),
        ],
        out_specs=pl.BlockSpec((TM_MOE, d), lambda i, e: (i, 0)),
        out_shape=jax.ShapeDtypeStruct((tp, d), F32),
        scratch_shapes=[pltpu.VMEM((TM_MOE, d), F32), pltpu.VMEM((TM_MOE, d), BF16)],
        compiler_params=pltpu.CompilerParams(dimension_semantics=("arbitrary", "arbitrary"),
                                             vmem_limit_bytes=VMEM_LIMIT),
        name="moe",
    )(h, comb, wgu, wd, g, b)


def _rope_tables(pos):
    half = HEAD_DIM // 2
    inv = jnp.power(jnp.float32(ROPE_THETA), -jnp.arange(half, dtype=F32) * 2.0 / HEAD_DIM)
    ang = pos.astype(F32)[:, None] * inv[None, :]
    cos, sin = jnp.cos(ang), jnp.sin(ang)
    reps = LANES // HEAD_DIM
    return (jnp.concatenate([cos, cos] * reps, axis=1), jnp.concatenate([-sin, sin] * reps, axis=1))


def _pad_rows(a, rows):
    return jnp.pad(a, [(0, rows - a.shape[0])] + [(0, 0)] * (a.ndim - 1))


def kernel(x_prompt, x_sample, cache_k, cache_v, cache_idx_k, state_conv, page_table, meta_tokens, w_in, b_in, w_conv, w_attn_up, w_conv_out, w_o, ln1_g, ln1_b, w_group, b_group, w_expert_router, b_expert_router, w_gate, w_up, w_down, ln2_g, ln2_b):
    bsz, s_p, d = x_prompt.shape
    db, s_s, _ = x_sample.shape
    depth = w_in.shape[0]
    assert bsz == 1, "prompt group is served one sequence at a time"
    assert s_s >= CONV_WIDTH - 1
    n_phys, page = cache_k.shape[1], cache_k.shape[2]
    pages = page_table.shape[1]
    past_len = pages * page
    t_p = s_p + N_META
    t_s = db * s_s
    t_all = t_p + t_s
    topk_p = min(TOPK_MAX, t_p // 4)
    topk_s = min(TOPK_MAX, (past_len + s_s) // 4)
    alpha = (2 * depth) ** 0.25

    d_attn = N_HEADS * HEAD_DIM
    d_kv = KV_HEADS * HEAD_DIM
    d_idx = IDX_HEADS * IDX_DIM
    d_conv = w_conv.shape[2]
    segs, offs, _ = _proj_layout(d_attn, d_kv, d_idx, d_conv, d)
    widths = {name: width for name, width, _ in segs}
    ref_order = ["q", "k", "v", "iq", "iw", "ik", "cu", "cb", "cc", "ga", "gb"]
    ref_starts = np.concatenate([[0], np.cumsum([widths[nm] for nm in ref_order])])

    tq = -(-t_p // BQ) * BQ
    tk = -(-tq // LB) * LB
    tile = max(TM, TM_MOE, LB)
    assert tq % TM == 0 and TM % s_s == 0
    tp = -(-max(tq + t_s, tk) // tile) * tile
    nqb = tq // BQ
    sl = slice(tq, tq + t_s)

    pos = jnp.concatenate([jnp.arange(t_p, dtype=jnp.int32), jnp.zeros((tq - t_p,), jnp.int32),
                           jnp.tile(past_len + jnp.arange(s_s, dtype=jnp.int32), db),
                           jnp.zeros((tp - tq - t_s,), jnp.int32)])
    cos_t, sin_t = _rope_tables(pos)

    h = jnp.concatenate([meta_tokens.astype(x_prompt.dtype), x_prompt[0], jnp.zeros((tq - t_p, d), x_prompt.dtype),
                         x_sample.reshape(t_s, d), jnp.zeros((tp - tq - t_s, d), x_prompt.dtype)], axis=0)

    q_scale = HEAD_DIM ** -0.5 * math.log2(math.e)

    outs = {k: [] for k in ("kp", "vp", "ikp", "cp", "ks", "vs", "iks", "cs")}
    for l in range(depth):
        pieces_w, pieces_b = [], []
        for nm, width, padded in segs:
            a = int(ref_starts[ref_order.index(nm)])
            pieces_w.append(jnp.pad(w_in[l][:, a:a + width], [(0, 0), (0, padded - width)]))
            pieces_b.append(jnp.pad(b_in[l][a:a + width], [(0, padded - width)]))
        w_p = jnp.concatenate(pieces_w, axis=1).astype(BF16)
        b_p = jnp.concatenate(pieces_b)[None, :]

        (q_r, iq_r, k_r, v_r, ik_r, kt_b, ikt_b, v_b, iw_r, z, cb, ga, gb) = _proj(
            h, w_p, b_p, cos_t, sin_t, offs, widths, q_scale)

        attn_p = _pattn(q_r, iq_r, iw_r, ikt_b, kt_b, v_b.reshape(tp // LB, LB, d_kv), topk_p, nqb)

        iq_s = iq_r[sl].astype(BF16).reshape(db, s_s * IDX_HEADS, IDX_DIM)
        iw_rep = jnp.broadcast_to(iw_r[sl].reshape(db, s_s * IDX_HEADS, 1), (db, s_s * IDX_HEADS, LANES))
        q_s = q_r[sl].astype(BF16).reshape(db, s_s, KV_HEADS, Q_PER_KV, HEAD_DIM)
        q_s = q_s.transpose(0, 2, 3, 1, 4)
        eye = jnp.eye(KV_HEADS, dtype=BF16)
        qbd = (q_s[:, :, :, :, None, :] * eye[None, :, None, None, :, None])
        qbd = qbd.reshape(db, KV_HEADS * Q_PER_KV * s_s, d_kv)
        new_page_t = lambda a: jnp.pad(a[sl].reshape(db, s_s, a.shape[1]).transpose(0, 2, 1),
                                       [(0, 0), (0, 0), (0, page - s_s)])
        cache_ikt = cache_idx_k[l].transpose(0, 2, 1)
        cache_kt = cache_k[l].transpose(0, 2, 3, 1).reshape(n_phys, d_kv, page)
        cache_vt = cache_v[l].transpose(0, 2, 3, 1).reshape(n_phys, d_kv, page)
        r_s = _sattn(page_table, cache_ikt, cache_kt, cache_vt, iq_s, iw_rep, qbd,
                     new_page_t(ik_r), new_page_t(k_r), new_page_t(v_r), topk_s)
        r_s = r_s.reshape(db, KV_HEADS, Q_PER_KV, s_s, KV_HEADS, HEAD_DIM)
        attn_s = jnp.stack([r_s[:, kvh, :, :, kvh, :] for kvh in range(KV_HEADS)], axis=1)
        attn_s = attn_s.transpose(0, 3, 1, 2, 4).reshape(t_s, d_attn)

        state = state_conv[l].astype(F32)
        assert CONV_WIDTH == 3
        st1 = jnp.broadcast_to(state[:, 1:2], (db, s_s, d_conv)).reshape(t_s, d_conv)
        st2 = jnp.concatenate([state[:, 0:1], jnp.broadcast_to(state[:, 1:2], (db, s_s - 1, d_conv))], axis=1)
        st2 = st2.reshape(t_s, d_conv)

        wr = jnp.pad(jnp.concatenate([w_group[l], w_expert_router[l]], axis=1),
                     [(0, 0), (0, LANES - N_GROUPS - N_EXPERTS)])
        br = jnp.pad(jnp.concatenate([b_group[l], b_expert_router[l]]), [(0, LANES - N_GROUPS - N_EXPERTS)])[None]
        h1, comb = _merge(h, attn_p, _pad_rows(attn_s, tp - tq), z, _pad_rows(st1, tp - tq),
                          _pad_rows(st2, tp - tq), cb, ga, gb, w_conv[l],
                          w_attn_up[l].astype(BF16), w_conv_out[l].astype(BF16), w_o[l].astype(BF16),
                          ln1_g[l][None], ln1_b[l][None], wr, br, alpha, s_s)

        wgu = jnp.concatenate([w_gate[l], w_up[l]], axis=2).astype(BF16)
        h = _moe(h1, comb, wgu, w_down[l].astype(BF16), ln2_g[l][None], ln2_b[l][None], alpha)

        outs["kp"].append(k_r[:t_p].reshape(bsz, t_p, KV_HEADS, HEAD_DIM))
        outs["vp"].append(v_r[:t_p].reshape(bsz, t_p, KV_HEADS, HEAD_DIM))
        outs["ikp"].append(ik_r[:t_p].reshape(bsz, t_p, IDX_DIM))
        outs["cp"].append(z[t_p - (CONV_WIDTH - 1):t_p].reshape(bsz, CONV_WIDTH - 1, d_conv))
        outs["ks"].append(k_r[sl].reshape(db, s_s, KV_HEADS, HEAD_DIM))
        outs["vs"].append(v_r[sl].reshape(db, s_s, KV_HEADS, HEAD_DIM))
        outs["iks"].append(ik_r[sl].reshape(db, s_s, IDX_DIM))
        outs["cs"].append(z[sl].reshape(db, s_s, d_conv)[:, -(CONV_WIDTH - 1):])

    y_prompt = h[N_META:t_p].reshape(bsz, s_p, d)
    y_sample = h[sl].reshape(db, s_s, d)
    st = lambda k: jnp.stack(outs[k])
    return (y_prompt, y_sample, st("kp"), st("vp"), st("ikp"), st("cp"), st("ks"), st("vs"), st("iks"), st("cs"))
```

```python
import functools
import math

import numpy as np
import jax
import jax.numpy as jnp
from jax import lax
from jax.experimental import pallas as pl
from jax.experimental.pallas import tpu as pltpu

F32 = jnp.float32
BF16 = jnp.bfloat16
I32 = jnp.int32

N_META = 16
N_HEADS = 8
HEAD_DIM = 64
KV_HEADS = 4
Q_PER_KV = N_HEADS // KV_HEADS
IDX_HEADS = 8
IDX_DIM = 64
TOPK_MAX = 256
CONV_WIDTH = 3
N_GROUPS = 4
EXPERTS_PER_GROUP = 4
N_EXPERTS = N_GROUPS * EXPERTS_PER_GROUP
ROPE_THETA = 10000.0
LN_EPS = 1e-5
IDX_W_SCALE = (IDX_HEADS ** -0.5) * (IDX_DIM ** -0.5)

LANES = 128
SUBLANES = 8
INT_MIN = -(2 ** 31)
KEY_NEG_FLT_MAX = INT_MIN + (1 << 23)
FLT_MAX = float(np.finfo(np.float32).max)
NEG = -1e30
VMEM_LIMIT = 56 * 1024 * 1024

TM = 256
TM_MOE = 1024
BQ = 256
LB = 512
ROW_GROUP = 128
PAGES_PER_STEP = 16
SEQS_PER_GROUP = 4


def _tile_lanes(x, n):
    reps = n // x.shape[1]
    return x if reps == 1 else jnp.concatenate([x] * reps, axis=1)


def _layer_norm(x, g, b):
    mu = jnp.mean(x, axis=-1, keepdims=True)
    xc = x - mu
    var = jnp.mean(xc * xc, axis=-1, keepdims=True)
    return xc * lax.rsqrt(var + LN_EPS) * g + b


def _proj_layout(d_attn, d_kv, d_idx, d_conv, d_model):
    segs = [("q", d_attn, d_attn), ("k", d_kv, d_kv), ("v", d_kv, d_kv), ("iq", d_idx, d_idx),
            ("iw", IDX_HEADS, LANES), ("ik", IDX_DIM, LANES),
            ("cu", d_conv, d_conv), ("cb", d_conv, d_conv), ("cc", d_conv, d_conv),
            ("ga", d_model, d_model), ("gb", d_model, d_model)]
    offs, o = {}, 0
    for name, _, padded in segs:
        offs[name] = (o, padded)
        o += padded
    return segs, offs, o


def _proj_kernel(x_ref, w_ref, b_ref, cos_ref, sin_ref,
                 q_ref, iq_ref, k_ref, v_ref, ik_ref, kt_ref, ikt_ref, vb_ref, iw_ref, z_ref, cb_ref, ga_ref, gb_ref,
                 *, offs, q_scale):
    xb = x_ref[...].astype(BF16)

    def seg(name):
        a, n = offs[name]
        return jnp.dot(xb, w_ref[:, a:a + n], preferred_element_type=F32) + b_ref[:, a:a + n]

    cos = cos_ref[...]
    sin = sin_ref[...]

    def rope(y):
        n = y.shape[1]
        lane = lax.broadcasted_iota(I32, y.shape, 1)
        first_half = (lane & (HEAD_DIM - 1)) < (HEAD_DIM // 2)
        swapped = jnp.where(first_half, pltpu.roll(y, n - HEAD_DIM // 2, 1), pltpu.roll(y, HEAD_DIM // 2, 1))
        return y * _tile_lanes(cos, n) + swapped * _tile_lanes(sin, n)

    q_ref[...] = rope(seg("q")) * q_scale
    iq_ref[...] = rope(seg("iq"))
    k = rope(seg("k"))
    k_ref[...] = k
    kt_ref[...] = k.T.astype(BF16)
    v = seg("v")
    v_ref[...] = v
    vb_ref[...] = v.astype(BF16)
    ik = rope(seg("ik"))
    ik_ref[...] = ik[:, :IDX_DIM]
    ikt_ref[...] = ik.T[:IDX_DIM].astype(BF16)
    iw_ref[...] = seg("iw")[:, :IDX_HEADS] * IDX_W_SCALE
    z_ref[...] = seg("cc") * seg("cu")
    cb_ref[...] = seg("cb")
    ga_ref[...] = seg("ga")
    gb_ref[...] = seg("gb")


def _proj(x, w, b, cos, sin, offs, widths, q_scale):
    tp, d = x.shape
    n = w.shape[1]
    tm = LB
    nb = tp // tm
    row = lambda width: pl.BlockSpec((tm, width), lambda i: (i, 0))
    blk = lambda depth: pl.BlockSpec((None, depth, tm), lambda i: (i, 0, 0))
    const = lambda shape: pl.BlockSpec(shape, lambda i: (0, 0), pipeline_mode=pl.Buffered(1))
    dq, dkv, dc, dm = widths["q"], widths["k"], widths["cu"], widths["ga"]
    f32 = lambda width: jax.ShapeDtypeStruct((tp, width), F32)
    out = [(row(dq), f32(dq)), (row(widths["iq"]), f32(widths["iq"])),
           (row(dkv), f32(dkv)), (row(dkv), f32(dkv)), (row(IDX_DIM), f32(IDX_DIM)),
           (blk(dkv), jax.ShapeDtypeStruct((nb, dkv, tm), BF16)),
           (blk(IDX_DIM), jax.ShapeDtypeStruct((nb, IDX_DIM, tm), BF16)),
           (row(dkv), jax.ShapeDtypeStruct((tp, dkv), BF16)),
           (row(IDX_HEADS), f32(IDX_HEADS)), (row(dc), f32(dc)), (row(dc), f32(dc)),
           (row(dm), f32(dm)), (row(dm), f32(dm))]
    return pl.pallas_call(
        functools.partial(_proj_kernel, offs=offs, q_scale=q_scale),
        grid=(nb,),
        in_specs=[row(d), const((d, n)), const((1, n)), row(LANES), row(LANES)],
        out_specs=[o[0] for o in out],
        out_shape=[o[1] for o in out],
        compiler_params=pltpu.CompilerParams(dimension_semantics=("arbitrary",), vmem_limit_bytes=VMEM_LIMIT),
        name="proj",
    )(x, w, b, cos, sin)


def _key_to_float(c):
    bits = jnp.where(c >= 0, c, c ^ jnp.int32(0x7FFFFFFF))
    return lax.bitcast_convert_type(bits, F32)


def _float_to_key(x):
    bits = lax.bitcast_convert_type(x, I32)
    return jnp.where(bits >= 0, bits, bits ^ jnp.int32(0x7FFFFFFF))


def _select_topk(s_ref, thr_ref, ntiles, topk, idx_bits, rg):
    _, nslab, rows, _ = s_ref.shape
    width = nslab * LANES
    nbin = -(-topk // LANES)
    assert nbin <= nslab and rows % rg == 0
    static = isinstance(ntiles, int)

    def loop(body, init):
        if not static:
            return lax.fori_loop(0, ntiles, body, init)
        carry = init
        for t in range(ntiles):
            carry = body(t, carry)
        return carry

    for r0 in range(0, rows, rg):
        lane = lax.broadcasted_iota(I32, (rg, LANES), 1)
        slab = lambda t, j: s_ref[t, j, r0:r0 + rg, :]

        def count(pred):
            def body(t, cnt):
                for j in range(nslab):
                    cnt = cnt + jnp.where(pred(slab(t, j), t * width + j * LANES + lane), 1.0, 0.0)
                return cnt
            return jnp.sum(loop(body, jnp.zeros((rg, LANES), F32)), axis=1, keepdims=True)

        def bin_body(t, bm):
            bm = list(bm)
            for j in range(nslab):
                bm[j % nbin] = jnp.maximum(bm[j % nbin], slab(t, j))
            return tuple(bm)

        bm = loop(bin_body, tuple(jnp.full((rg, LANES), -jnp.inf, F32) for _ in range(nbin)))
        lo, hi = bm[0], bm[0]
        for j in range(1, nbin):
            lo, hi = jnp.minimum(lo, bm[j]), jnp.maximum(hi, bm[j])
        lo_f = jnp.broadcast_to(jnp.min(lo, axis=1, keepdims=True), (rg, LANES))
        hi_f = jnp.broadcast_to(jnp.max(hi, axis=1, keepdims=True), (rg, LANES))

        def unfinished(lo_k, c_lo, hi_k):
            open_ = jnp.where(c_lo == topk, 0.0, jnp.where(hi_k - 1 > lo_k, 1.0, 0.0))
            return (jnp.max(open_) > 0.5).astype(I32)

        def cond(st):
            return st[0] > 0

        def body(st):
            _, lo_k, c_lo, hi_k, c_hi = st
            mid = (lo_k >> 1) + (hi_k >> 1) + (lo_k & hi_k & 1)
            mid_f = _key_to_float(mid)
            cnt = count(lambda s, idx: s >= mid_f)
            ge = cnt >= topk
            lo_k, c_lo = jnp.where(ge, mid, lo_k), jnp.where(ge, cnt, c_lo)
            hi_k, c_hi = jnp.where(ge, hi_k, mid), jnp.where(ge, c_hi, cnt)
            return unfinished(lo_k, c_lo, hi_k), lo_k, c_lo, hi_k, c_hi

        lo_k0 = _float_to_key(lo_f)
        hi_k0 = _float_to_key(hi_f) + 1
        c_lo0 = jnp.broadcast_to(count(lambda s, idx: s >= lo_f), (rg, LANES))
        _, lo_k, c_lo, _, c_hi = lax.while_loop(
            cond, body, (unfinished(lo_k0, c_lo0, hi_k0), lo_k0, c_lo0, hi_k0, jnp.zeros((rg, LANES), F32)))
        finite = lo_k >= KEY_NEG_FLT_MAX
        thr = jnp.where(finite, _key_to_float(lo_k), -FLT_MAX)
        thr_ref[r0:r0 + rg, :] = thr

        tied = jnp.where(finite, jnp.where(c_lo > topk, 1.0, 0.0), 0.0)

        @pl.when(jnp.max(tied) > 0.5)
        def _():
            need = jnp.where(tied > 0.5, topk - c_hi, float(2 ** 30))

            def index_bit(bi, x):
                cand = x + lax.shift_left(jnp.int32(1), idx_bits - 1 - bi)
                cnt = count(lambda s, idx: jnp.where(s == thr, idx, cand) < cand)
                return jnp.where(cnt < need, cand, x)

            x = lax.fori_loop(0, idx_bits, index_bit, jnp.zeros((rg, LANES), I32))

            def fix(t_, c):
                for j in range(nslab):
                    sj = slab(t_, j)
                    drop = jnp.where(sj == thr, t_ * width + j * LANES + lane, x) > x
                    s_ref[t_, j, r0:r0 + rg, :] = jnp.where(drop, -jnp.inf, sj)
                return c

            loop(fix, 0)


def _pattn_kernel(q_ref, iq_ref, iw_ref, ikt_ref, kt_ref, v_ref, o_ref,
                  s_ref, thr_ref, wb_ref, qg_ref, iqh_ref, m_ref, l_ref, acc_ref, *, topk, idx_bits):
    bq = o_ref.shape[0]
    lb = s_ref.shape[1] * LANES
    i = pl.program_id(0)
    nkb = pl.cdiv((i + 1) * bq, lb)

    for h in range(IDX_HEADS):
        wb_ref[h] = jnp.broadcast_to(iw_ref[:, h:h + 1], (bq, LANES))
        iqh_ref[h * bq:(h + 1) * bq, :] = iq_ref[:, h * IDX_DIM:(h + 1) * IDX_DIM].astype(BF16)
    for h in range(N_HEADS):
        kvh, g = divmod(h, Q_PER_KV)
        qg_ref[kvh, g * bq:(g + 1) * bq, :] = q_ref[:, h * HEAD_DIM:(h + 1) * HEAD_DIM].astype(BF16)

    row = lax.broadcasted_iota(I32, (bq, lb), 0)
    col = lax.broadcasted_iota(I32, (bq, lb), 1)

    def scores(kb, c, causal):
        d = jnp.dot(iqh_ref[...], ikt_ref[kb], preferred_element_type=F32)
        sc = jnp.zeros((bq, lb), F32)
        for h in range(IDX_HEADS):
            sc = sc + _tile_lanes(wb_ref[h], lb) * jnp.maximum(d[h * bq:(h + 1) * bq], 0.0)
        if causal:
            sc = jnp.where(kb * lb + col <= i * bq + row, sc, -jnp.inf)
        for j in range(lb // LANES):
            s_ref[kb, j] = sc[:, j * LANES:(j + 1) * LANES]
        return c

    nfull = (i * bq + 1) // lb
    lax.fori_loop(0, nfull, functools.partial(scores, causal=False), 0)
    lax.fori_loop(nfull, nkb, functools.partial(scores, causal=True), 0)

    _select_topk(s_ref, thr_ref, nkb, topk, idx_bits, ROW_GROUP)
    thr_full = _tile_lanes(thr_ref[...], lb)

    m_ref[...] = jnp.full(m_ref.shape, NEG, F32)
    l_ref[...] = jnp.zeros(l_ref.shape, F32)
    acc_ref[...] = jnp.zeros(acc_ref.shape, F32)

    def attend(kb, c):
        sc = jnp.concatenate([s_ref[kb, j] for j in range(lb // LANES)], axis=1)
        bias = jnp.where(sc >= thr_full, 0.0, NEG)
        bias2 = jnp.concatenate([bias] * Q_PER_KV, axis=0)
        for kvh in range(KV_HEADS):
            s = jnp.dot(qg_ref[kvh], kt_ref[kb, kvh * HEAD_DIM:(kvh + 1) * HEAD_DIM, :],
                        preferred_element_type=F32) + bias2
            m_prev = m_ref[kvh]
            m_new = jnp.maximum(m_prev, jnp.max(s, axis=1, keepdims=True))
            alpha = jnp.exp2(m_prev - m_new)
            p = jnp.exp2(s - _tile_lanes(m_new, lb))
            psum = p[:, :LANES]
            for j in range(1, lb // LANES):
                psum = psum + p[:, j * LANES:(j + 1) * LANES]
            l_ref[kvh] = alpha * l_ref[kvh] + psum
            pair = kvh // 2
            pv = jnp.dot(p.astype(BF16), v_ref[kb, :, pair * LANES:(pair + 1) * LANES],
                         preferred_element_type=F32)
            acc_ref[kvh] = alpha * acc_ref[kvh] + pv
            m_ref[kvh] = m_new
        return c

    lax.fori_loop(0, nkb, attend, 0)

    for kvh in range(KV_HEADS):
        denom = jnp.sum(l_ref[kvh], axis=1, keepdims=True)
        off = (kvh % 2) * HEAD_DIM
        o = acc_ref[kvh][:, off:off + HEAD_DIM] / denom
        for g in range(Q_PER_KV):
            h = kvh * Q_PER_KV + g
            o_ref[:, h * HEAD_DIM:(h + 1) * HEAD_DIM] = o[g * bq:(g + 1) * bq]


def _pattn(q, iq, iw, ikt, kt, vv, topk, nqb):
    bq = BQ
    rows2 = Q_PER_KV * bq
    lb = ikt.shape[2]
    tq = nqb * bq
    nkb = pl.cdiv(tq, lb)
    assert nkb <= ikt.shape[0]
    idx_bits = int(nkb * lb).bit_length()
    resident = lambda shape: pl.BlockSpec(shape, lambda i: (0,) * len(shape), pipeline_mode=pl.Buffered(1))
    return pl.pallas_call(
        functools.partial(_pattn_kernel, topk=topk, idx_bits=idx_bits),
        grid=(nqb,),
        in_specs=[
            pl.BlockSpec((bq, q.shape[1]), lambda i: (i, 0)),
            pl.BlockSpec((bq, iq.shape[1]), lambda i: (i, 0)),
            pl.BlockSpec((bq, IDX_HEADS), lambda i: (i, 0)),
            resident(ikt.shape), resident(kt.shape), resident(vv.shape),
        ],
        out_specs=pl.BlockSpec((bq, N_HEADS * HEAD_DIM), lambda i: (i, 0)),
        out_shape=jax.ShapeDtypeStruct((tq, N_HEADS * HEAD_DIM), F32),
        scratch_shapes=[
            pltpu.VMEM((nkb, lb // LANES, bq, LANES), F32),
            pltpu.VMEM((bq, LANES), F32),
            pltpu.VMEM((IDX_HEADS, bq, LANES), F32),
            pltpu.VMEM((KV_HEADS, rows2, HEAD_DIM), BF16),
            pltpu.VMEM((IDX_HEADS * bq, IDX_DIM), BF16),
            pltpu.VMEM((KV_HEADS, rows2, LANES), F32),
            pltpu.VMEM((KV_HEADS, rows2, LANES), F32),
            pltpu.VMEM((KV_HEADS, rows2, LANES), F32),
        ],
        compiler_params=pltpu.CompilerParams(dimension_semantics=("arbitrary",), vmem_limit_bytes=VMEM_LIMIT),
        name="pattn",
    )(q, iq, iw, ikt, kt, vv)


def _sattn_kernel(pt_ref, *refs, topk, idx_bits, s_new):
    del pt_ref
    np_ = PAGES_PER_STEP
    cik = refs[0:np_]
    ck = refs[np_:2 * np_]
    cv = refs[2 * np_:3 * np_]
    iq_ref, iw_ref, qbd_ref, nik_ref, nk_ref, nv_ref, o_ref, s_ref, thr_ref, lg_ref, vst_ref = refs[3 * np_:]
    g = pl.program_id(1)
    jj = pl.program_id(2)
    ngrp, nsteps = pl.num_programs(1), pl.num_programs(2)
    nseq = o_ref.shape[0]
    ntiles, _, rows_all, page = s_ref.shape
    rows = rows_all // nseq
    width = np_ * page
    reps = rows // s_new
    row0 = pl.multiple_of(g * rows, rows)

    @pl.when(jnp.logical_and(pl.program_id(0) == 0, jnp.logical_and(g == 0, jj == 0)))
    def _():
        s_ref[ntiles - 1] = jnp.full(s_ref.shape[1:], -jnp.inf, F32)
        lg_ref[:, ntiles - 1] = jnp.zeros((nseq,) + lg_ref.shape[2:], F32)
        vst_ref[:, ntiles - 1] = jnp.zeros((nseq,) + vst_ref.shape[2:], BF16)

    def do_page(t, p, ikt, kt, vt, is_new):
        d = jnp.dot(iq_ref[...], ikt.astype(BF16), preferred_element_type=F32)
        r = jnp.maximum(d, 0.0) * iw_ref[...]
        per_q = [jnp.sum(r[q * IDX_HEADS:(q + 1) * IDX_HEADS], axis=0, keepdims=True) for q in range(s_new)]
        sc = jnp.concatenate(per_q * reps, axis=0)
        if is_new:
            qi = lax.broadcasted_iota(I32, sc.shape, 0) & (s_new - 1)
            ki = lax.broadcasted_iota(I32, sc.shape, 1)
            sc = jnp.where(ki <= qi, sc, -jnp.inf)
        s_ref[t, p, pl.ds(row0, rows), :] = sc
        lg_ref[g, t, :, p * page:(p + 1) * page] = jnp.dot(qbd_ref[...], kt.astype(BF16),
                                                           preferred_element_type=F32)
        vst_ref[g, t, :, p * page:(p + 1) * page] = vt.astype(BF16)

    for p in range(np_):
        do_page(jj, p, cik[p][...], ck[p][...], cv[p][...], False)

    @pl.when(jj == nsteps - 1)
    def _():
        do_page(ntiles - 1, 0, nik_ref[...], nk_ref[...], nv_ref[...], True)

    @pl.when(jnp.logical_and(g == ngrp - 1, jj == nsteps - 1))
    def _():
        _select_topk(s_ref, thr_ref, ntiles, topk, idx_bits, rows_all)
        nvreg = lg_ref.shape[2] // rows
        for q in range(nseq):
            thr_full = _tile_lanes(thr_ref[q * rows:(q + 1) * rows, :], width)

            def masked(t):
                sc = jnp.concatenate([s_ref[t, j, q * rows:(q + 1) * rows, :] for j in range(np_)], axis=1)
                bias = jnp.where(sc >= thr_full, 0.0, NEG)
                return lg_ref[q, t] + jnp.concatenate([bias] * nvreg, axis=0)

            m = jnp.max(masked(0), axis=1, keepdims=True)
            for t in range(1, ntiles):
                m = jnp.maximum(m, jnp.max(masked(t), axis=1, keepdims=True))
            l = jnp.zeros((lg_ref.shape[2], 1), F32)
            acc = jnp.zeros(o_ref.shape[1:], F32)
            for t in range(ntiles):
                p = jnp.exp2(masked(t) - m)
                l = l + jnp.sum(p, axis=1, keepdims=True)
                acc = acc + lax.dot_general(p.astype(BF16), vst_ref[q, t], (((1,), (1,)), ((), ())),
                                            preferred_element_type=F32)
            o_ref[q] = acc / l


def _sattn(page_table, cache_ikt, cache_kt, cache_vt, iq_s, iw_rep, qbd, nikt, nkt, nvt, topk):
    db, pages = page_table.shape
    page = cache_ikt.shape[2]
    assert page == LANES and pages % PAGES_PER_STEP == 0
    s_new = iq_s.shape[1] // IDX_HEADS
    rows = SUBLANES
    assert rows % s_new == 0 and s_new & (s_new - 1) == 0
    nseq = math.gcd(db, SEQS_PER_GROUP)
    nrow = qbd.shape[1]
    dkv = cache_kt.shape[1]
    nsteps = pages // PAGES_PER_STEP
    width = PAGES_PER_STEP * page
    idx_bits = int((nsteps + 1) * width).bit_length()

    def page_spec(depth, p):
        return pl.BlockSpec(
            (None, depth, page),
            lambda bo, g, jj, pt, p=p: (pt[(bo * nseq + g) * pages + jj * PAGES_PER_STEP + p], 0, 0))

    per_seq = lambda shape: pl.BlockSpec((None,) + shape, lambda bo, g, jj, pt: (bo * nseq + g, 0, 0))
    in_specs = ([page_spec(IDX_DIM, p) for p in range(PAGES_PER_STEP)]
                + [page_spec(dkv, p) for p in range(PAGES_PER_STEP)]
                + [page_spec(dkv, p) for p in range(PAGES_PER_STEP)]
                + [per_seq(iq_s.shape[1:]), per_seq(iw_rep.shape[1:]), per_seq(qbd.shape[1:]),
                   per_seq(nikt.shape[1:]), per_seq(nkt.shape[1:]), per_seq(nvt.shape[1:])])
    grid_spec = pltpu.PrefetchScalarGridSpec(
        num_scalar_prefetch=1,
        grid=(db // nseq, nseq, nsteps),
        in_specs=in_specs,
        out_specs=pl.BlockSpec((nseq, nrow, dkv), lambda bo, g, jj, pt: (bo, 0, 0)),
        scratch_shapes=[
            pltpu.VMEM((nsteps + 1, PAGES_PER_STEP, nseq * rows, page), F32),
            pltpu.VMEM((nseq * rows, LANES), F32),
            pltpu.VMEM((nseq, nsteps + 1, nrow, width), F32),
            pltpu.VMEM((nseq, nsteps + 1, dkv, width), BF16),
        ],
    )
    args = ([cache_ikt] * PAGES_PER_STEP + [cache_kt] * PAGES_PER_STEP + [cache_vt] * PAGES_PER_STEP
            + [iq_s, iw_rep, qbd, nikt, nkt, nvt])
    return pl.pallas_call(
        functools.partial(_sattn_kernel, topk=topk, idx_bits=idx_bits, s_new=s_new),
        grid_spec=grid_spec,
        out_shape=jax.ShapeDtypeStruct((db, nrow, dkv), F32),
        compiler_params=pltpu.CompilerParams(dimension_semantics=("arbitrary", "arbitrary", "arbitrary"),
                                             vmem_limit_bytes=VMEM_LIMIT),
        name="sattn",
    )(page_table.reshape(-1), *args)


def _merge_kernel(x_ref, ap_ref, as_ref, z_ref, zh_ref, st1_ref, st2_ref, cb_ref, ga_ref, gb_ref,
                  wc_ref, wa_ref, wb_ref, wo_ref, g_ref, b_ref, wr_ref, br_ref,
                  h_ref, comb_ref, *, alpha, n_prompt_tiles, s_new):
    i = pl.program_id(0)
    is_prompt = i < n_prompt_tiles
    z = z_ref[...]
    row = lax.broadcasted_iota(I32, z.shape, 0)
    halo = jnp.where(i == 0, 0.0, zh_ref[...])
    prev1 = jnp.broadcast_to(halo[SUBLANES - 1:SUBLANES], z.shape)
    prev2 = jnp.broadcast_to(halo[SUBLANES - 2:SUBLANES - 1], z.shape)
    seq_row = row & (s_new - 1)
    first = jnp.where(is_prompt, row, seq_row)
    fill1 = jnp.where(is_prompt, prev1, st1_ref[...])
    fill2 = jnp.where(is_prompt, jnp.where(row == 0, prev2, prev1), st2_ref[...])
    z1 = jnp.where(first >= 1, pltpu.roll(z, 1, 0), fill1)
    z2 = jnp.where(first >= 2, pltpu.roll(z, 2, 0), fill2)
    wc = wc_ref[...]
    y = wc[0:1] * z2 + wc[1:2] * z1 + wc[2:3] * z
    conv_o = cb_ref[...] * y
    attn = jnp.where(is_prompt, ap_ref[...], as_ref[...])
    a = jnp.dot(attn.astype(BF16), wa_ref[...], preferred_element_type=F32)
    b = jnp.dot(conv_o.astype(BF16), wb_ref[...], preferred_element_type=F32)
    sig = lambda u: 1.0 / (1.0 + jnp.exp(-u))
    mixed = sig(ga_ref[...]) * a + sig(gb_ref[...]) * b
    mix = jnp.dot(mixed.astype(BF16), wo_ref[...], preferred_element_type=F32)
    h = _layer_norm(alpha * x_ref[...] + mix, g_ref[...], b_ref[...])
    h_ref[...] = h

    logits = jnp.dot(h, wr_ref[...], preferred_element_type=F32, precision=lax.Precision.HIGHEST) + br_ref[...]
    lane_i = lax.broadcasted_iota(I32, logits.shape, 1)
    lane = lane_i.astype(F32)
    first_where = lambda cond: jnp.min(jnp.where(cond, lane, float(LANES)), axis=1, keepdims=True)
    is_g = lane_i < N_GROUPS
    gl = jnp.where(is_g, logits, -jnp.inf)
    gmax = jnp.max(gl, axis=1, keepdims=True)
    g_sel = first_where(gl == gmax)
    g_p = 1.0 / jnp.sum(jnp.where(is_g, jnp.exp(logits - gmax), 0.0), axis=1, keepdims=True)
    grp = jnp.where((lane_i >= N_GROUPS) & (lane_i < N_GROUPS + N_EXPERTS),
                    lax.shift_right_arithmetic(lane_i - N_GROUPS, jnp.int32(int(math.log2(EXPERTS_PER_GROUP)))),
                    -1).astype(F32)
    in_grp = grp == g_sel
    e1 = jnp.where(in_grp, logits, -jnp.inf)
    max1 = jnp.max(e1, axis=1, keepdims=True)
    i1 = first_where(e1 == max1)
    e2 = jnp.where(lane == i1, -jnp.inf, e1)
    max2 = jnp.max(e2, axis=1, keepdims=True)
    i2 = first_where(e2 == max2)
    den = jnp.sum(jnp.where(in_grp, jnp.exp(logits - max1), 0.0), axis=1, keepdims=True)
    p1 = 1.0 / den
    p2 = jnp.exp(max2 - max1) / den
    tot = p1 + p2
    comb_ref[...] = jnp.where(lane == i1, p1 / tot * g_p, 0.0) + jnp.where(lane == i2, p2 / tot * g_p, 0.0)


def _merge(x, attn_p, attn_s, z, st1, st2, cb, ga, gb, wc, wa, wb, wo, g, b, wr, br, alpha, s_new):
    tp, d = x.shape
    npt = attn_p.shape[0] // TM
    assert attn_p.shape[0] % TM == 0 and attn_s.shape[0] == tp - npt * TM and s_new & (s_new - 1) == 0
    row = lambda width: pl.BlockSpec((TM, width), lambda i: (i, 0))
    prompt_row = lambda width: pl.BlockSpec((TM, width), lambda i: (jnp.minimum(i, npt - 1), 0))
    sample_row = lambda width: pl.BlockSpec((TM, width), lambda i: (jnp.maximum(i - npt, 0), 0))
    halo = pl.BlockSpec((SUBLANES, z.shape[1]), lambda i: (jnp.maximum(i * (TM // SUBLANES) - 1, 0), 0))
    const = lambda a: pl.BlockSpec(a.shape, lambda i: (0, 0), pipeline_mode=pl.Buffered(1))
    dc = z.shape[1]
    return pl.pallas_call(
        functools.partial(_merge_kernel, alpha=alpha, n_prompt_tiles=npt, s_new=s_new),
        grid=(tp // TM,),
        in_specs=[row(d), prompt_row(attn_p.shape[1]), sample_row(attn_s.shape[1]), row(dc), halo,
                  sample_row(dc), sample_row(dc), row(dc), row(d), row(d),
                  const(wc), const(wa), const(wb), const(wo), const(g), const(b), const(wr), const(br)],
        out_specs=[row(d), row(LANES)],
        out_shape=[jax.ShapeDtypeStruct((tp, d), F32), jax.ShapeDtypeStruct((tp, LANES), F32)],
        compiler_params=pltpu.CompilerParams(dimension_semantics=("arbitrary",), vmem_limit_bytes=VMEM_LIMIT),
        name="merge",
    )(x, attn_p, attn_s, z, z, st1, st2, cb, ga, gb, wc, wa, wb, wo, g, b, wr, br)


def _moe_kernel(h_ref, comb_ref, wgu_ref, wd_ref, g_ref, b_ref, o_ref, acc_ref, hb_ref, *, alpha):
    step = pl.program_id(1)
    per_step, de = wd_ref.shape[0], wd_ref.shape[1]

    @pl.when(step == 0)
    def _():
        acc_ref[...] = jnp.zeros(acc_ref.shape, F32)
        hb_ref[...] = h_ref[...].astype(BF16)

    lane = lax.broadcasted_iota(I32, comb_ref.shape, 1)
    out = acc_ref[...]
    for k in range(per_step):
        e = step * per_step + k
        gu = jnp.dot(hb_ref[...], wgu_ref[k], preferred_element_type=F32)
        hg = gu[:, :de]
        hu = gu[:, de:]
        c = jnp.sum(jnp.where(lane == e + N_GROUPS, comb_ref[...], 0.0), axis=1, keepdims=True)
        act = hg * (1.0 / (1.0 + jnp.exp(-hg))) * hu * c
        out = out + jnp.dot(act.astype(BF16), wd_ref[k], preferred_element_type=F32)
    acc_ref[...] = out

    @pl.when(step == pl.num_programs(1) - 1)
    def _():
        o_ref[...] = _layer_norm(alpha * h_ref[...] + acc_ref[...], g_ref[...], b_ref[...])


def _moe(h, comb, wgu, wd, g, b, alpha):
    tp, d = h.shape
    ne, _, de2 = wgu.shape
    de = wd.shape[1]
    per_step = EXPERTS_PER_GROUP
    assert ne % per_step == 0
    return pl.pallas_call(
        functools.partial(_moe_kernel, alpha=alpha),
        grid=(tp // TM_MOE, ne // per_step),
        in_specs=[
            pl.BlockSpec((TM_MOE, d), lambda i, e: (i, 0)),
            pl.BlockSpec((TM_MOE, LANES), lambda i, e: (i, 0)),
            pl.BlockSpec((per_step, d, de2), lambda i, e: (e, 0, 0)),
            pl.BlockSpec((per_step, de, d), lambda i, e: (e, 0, 0)),
            pl.BlockSpec((1, d), lambda i, e: (0, 0)),
            pl.BlockSpec((1, d), lambda i, e: (0, 0)),
        ],
        out_specs=pl.BlockSpec((TM_MOE, d), lambda i, e: (i, 0)),
        out_shape=jax.ShapeDtypeStruct((tp, d), F32),
        scratch_shapes=[pltpu.VMEM((TM_MOE, d), F32), pltpu.VMEM((TM_MOE, d), BF16)],
        compiler_params=pltpu.CompilerParams(dimension_semantics=("arbitrary", "arbitrary"),
                                             vmem_limit_bytes=VMEM_LIMIT),
        name="moe",
    )(h, comb, wgu, wd, g, b)


def _rope_tables(pos):
    half = HEAD_DIM // 2
    inv = jnp.power(jnp.float32(ROPE_THETA), -jnp.arange(half, dtype=F32) * 2.0 / HEAD_DIM)
    ang = pos.astype(F32)[:, None] * inv[None, :]
    cos, sin = jnp.cos(ang), jnp.sin(ang)
    reps = LANES // HEAD_DIM
    return (jnp.concatenate([cos, cos] * reps, axis=1), jnp.concatenate([-sin, sin] * reps, axis=1))


def _pad_rows(a, rows):
    return jnp.pad(a, [(0, rows - a.shape[0])] + [(0, 0)] * (a.ndim - 1))


def kernel(x_prompt, x_sample, cache_k, cache_v, cache_idx_k, state_conv, page_table, meta_tokens, w_in, b_in, w_conv, w_attn_up, w_conv_out, w_o, ln1_g, ln1_b, w_group, b_group, w_expert_router, b_expert_router, w_gate, w_up, w_down, ln2_g, ln2_b):
    bsz, s_p, d = x_prompt.shape
    db, s_s, _ = x_sample.shape
    depth = w_in.shape[0]
    assert bsz == 1, "prompt group is served one sequence at a time"
    assert s_s >= CONV_WIDTH - 1
    n_phys, page = cache_k.shape[1], cache_k.shape[2]
    pages = page_table.shape[1]
    past_len = pages * page
    t_p = s_p + N_META
    t_s = db * s_s
    t_all = t_p + t_s
    topk_p = min(TOPK_MAX, t_p // 4)
    topk_s = min(TOPK_MAX, (past_len + s_s) // 4)
    alpha = (2 * depth) ** 0.25

    d_attn = N_HEADS * HEAD_DIM
    d_kv = KV_HEADS * HEAD_DIM
    d_idx = IDX_HEADS * IDX_DIM
    d_conv = w_conv.shape[2]
    segs, offs, _ = _proj_layout(d_attn, d_kv, d_idx, d_conv, d)
    widths = {name: width for name, width, _ in segs}
    ref_order = ["q", "k", "v", "iq", "iw", "ik", "cu", "cb", "cc", "ga", "gb"]
    ref_starts = np.concatenate([[0], np.cumsum([widths[nm] for nm in ref_order])])

    tq = -(-t_p // BQ) * BQ
    tk = -(-tq // LB) * LB
    tile = max(TM, TM_MOE, LB)
    assert tq % TM == 0 and TM % s_s == 0
    tp = -(-max(tq + t_s, tk) // tile) * tile
    nqb = tq // BQ
    sl = slice(tq, tq + t_s)

    pos = jnp.concatenate([jnp.arange(t_p, dtype=jnp.int32), jnp.zeros((tq - t_p,), jnp.int32),
                           jnp.tile(past_len + jnp.arange(s_s, dtype=jnp.int32), db),
                           jnp.zeros((tp - tq - t_s,), jnp.int32)])
    cos_t, sin_t = _rope_tables(pos)

    h = jnp.concatenate([meta_tokens.astype(x_prompt.dtype), x_prompt[0], jnp.zeros((tq - t_p, d), x_prompt.dtype),
                         x_sample.reshape(t_s, d), jnp.zeros((tp - tq - t_s, d), x_prompt.dtype)], axis=0)

    q_scale = HEAD_DIM ** -0.5 * math.log2(math.e)

    outs = {k: [] for k in ("kp", "vp", "ikp", "cp", "ks", "vs", "iks", "cs")}
    for l in range(depth):
        pieces_w, pieces_b = [], []
        for nm, width, padded in segs:
            a = int(ref_starts[ref_order.index(nm)])
            pieces_w.append(jnp.pad(w_in[l][:, a:a + width], [(0, 0), (0, padded - width)]))
            pieces_b.append(jnp.pad(b_in[l][a:a + width], [(0, padded - width)]))
        w_p = jnp.concatenate(pieces_w, axis=1).astype(BF16)
        b_p = jnp.concatenate(pieces_b)[None, :]

        (q_r, iq_r, k_r, v_r, ik_r, kt_b, ikt_b, v_b, iw_r, z, cb, ga, gb) = _proj(
            h, w_p, b_p, cos_t, sin_t, offs, widths, q_scale)

        attn_p = _pattn(q_r, iq_r, iw_r, ikt_b, kt_b, v_b.reshape(tp // LB, LB, d_kv), topk_p, nqb)

        iq_s = iq_r[sl].astype(BF16).reshape(db, s_s * IDX_HEADS, IDX_DIM)
        iw_rep = jnp.broadcast_to(iw_r[sl].reshape(db, s_s * IDX_HEADS, 1), (db, s_s * IDX_HEADS, LANES))
        q_s = q_r[sl].astype(BF16).reshape(db, s_s, KV_HEADS, Q_PER_KV, HEAD_DIM)
        q_s = q_s.transpose(0, 2, 3, 1, 4)
        eye = jnp.eye(KV_HEADS, dtype=BF16)
        qbd = (q_s[:, :, :, :, None, :] * eye[None, :, None, None, :, None])
        qbd = qbd.reshape(db, KV_HEADS * Q_PER_KV * s_s, d_kv)
        new_page_t = lambda a: jnp.pad(a[sl].reshape(db, s_s, a.shape[1]).transpose(0, 2, 1),
                                       [(0, 0), (0, 0), (0, page - s_s)])
        cache_ikt = cache_idx_k[l].transpose(0, 2, 1)
        cache_kt = cache_k[l].transpose(0, 2, 3, 1).reshape(n_phys, d_kv, page)
        cache_vt = cache_v[l].transpose(0, 2, 3, 1).reshape(n_phys, d_kv, page)
        r_s = _sattn(page_table, cache_ikt, cache_kt, cache_vt, iq_s, iw_rep, qbd,
                     new_page_t(ik_r), new_page_t(k_r), new_page_t(v_r), topk_s)
        r_s = r_s.reshape(db, KV_HEADS, Q_PER_KV, s_s, KV_HEADS, HEAD_DIM)
        attn_s = jnp.stack([r_s[:, kvh, :, :, kvh, :] for kvh in range(KV_HEADS)], axis=1)
        attn_s = attn_s.transpose(0, 3, 1, 2, 4).reshape(t_s, d_attn)

        state = state_conv[l].astype(F32)
        assert CONV_WIDTH == 3
        st1 = jnp.broadcast_to(state[:, 1:2], (db, s_s, d_conv)).reshape(t_s, d_conv)
        st2 = jnp.concatenate([state[:, 0:1], jnp.broadcast_to(state[:, 1:2], (db, s_s - 1, d_conv))], axis=1)
        st2 = st2.reshape(t_s, d_conv)

        wr = jnp.pad(jnp.concatenate([w_group[l], w_expert_router[l]], axis=1),
                     [(0, 0), (0, LANES - N_GROUPS - N_EXPERTS)])
        br = jnp.pad(jnp.concatenate([b_group[l], b_expert_router[l]]), [(0, LANES - N_GROUPS - N_EXPERTS)])[None]
        h1, comb = _merge(h, attn_p, _pad_rows(attn_s, tp - tq), z, _pad_rows(st1, tp - tq),
                          _pad_rows(st2, tp - tq), cb, ga, gb, w_conv[l],
                          w_attn_up[l].astype(BF16), w_conv_out[l].astype(BF16), w_o[l].astype(BF16),
                          ln1_g[l][None], ln1_b[l][None], wr, br, alpha, s_s)

        wgu = jnp.concatenate([w_gate[l], w_up[l]], axis=2).astype(BF16)
        h = _moe(h1, comb, wgu, w_down[l].astype(BF16), ln2_g[l][None], ln2_b[l][None], alpha)

        outs["kp"].append(k_r[:t_p].reshape(bsz, t_p, KV_HEADS, HEAD_DIM))
        outs["vp"].append(v_r[:t_p].reshape(bsz, t_p, KV_HEADS, HEAD_DIM))
        outs["ikp"].append(ik_r[:t_p].reshape(bsz, t_p, IDX_DIM))
        outs["cp"].append(z[t_p - (CONV_WIDTH - 1):t_p].reshape(bsz, CONV_WIDTH - 1, d_conv))
        outs["ks"].append(k_r[sl].reshape(db, s_s, KV_HEADS, HEAD_DIM))
        outs["vs"].append(v_r[sl].reshape(db, s_s, KV_HEADS, HEAD_DIM))
        outs["iks"].append(ik_r[sl].reshape(db, s_s, IDX_DIM))
        outs["cs"].append(z[sl].reshape(db, s_s, d_conv)[:, -(CONV_WIDTH - 1):])

    y_prompt = h[N_META:t_p].reshape(bsz, s_p, d)
    y_sample = h[sl].reshape(db, s_s, d)
    st = lambda k: jnp.stack(outs[k])
    return (y_prompt, y_sample, st("kp"), st("vp"), st("ikp"), st("cp"), st("ks"), st("vs"), st("iks"), st("cs"))
```
